```python
import math
import jax, jax.numpy as jnp
from jax import lax
import numpy as np

D_MODEL = 4096
BATCH = 8
SEQ = 2048
DEPTH = 2
DEC_BATCH = 1
DEC_SEQ = 16384
PAST_LEN = 128

NORM_EPS = 1e-6
ATT_HEAD_DIM = 128
ATT_Q_HEADS = 16
ATT_KV_HEADS = 4
ATT_GROUP = ATT_Q_HEADS // ATT_KV_HEADS
WINDOW = 128
BLOCK = 128
ROPE_THETA = 10000.0
Q_W = ATT_Q_HEADS * ATT_HEAD_DIM
KV_W = ATT_KV_HEADS * ATT_HEAD_DIM
HY_WIDTH = D_MODEL - Q_W
HY_SHORT = 3
HY_BANDS = 16
HY_EMB = 1 + 2 * HY_BANDS
HY_FFN = 64
HY_DECAY_TARGET = 1e-2
HY_FAST_DECAY = 0.3
HY_SLOW_DECAY = 1.5
HY_MOD_SHIFT = 0.05
IN_W_EVEN = Q_W + 2 * KV_W + 3 * HY_WIDTH
RW_HEAD = 64
RW_HEADS = D_MODEL // RW_HEAD
RW_DECAY_LORA = max(32, int(round(1.8 * math.sqrt(D_MODEL) / 32)) * 32)
RW_AAA_LORA = max(32, int(round(1.8 * math.sqrt(D_MODEL) / 32)) * 32)
RW_GATE_LORA = max(32, int(round(0.6 * D_MODEL ** 0.8 / 32)) * 32)
RW_GN_EPS = 64e-5
PEER_HEADS = 8
PEER_NKEYS = 128
PEER_EXPERTS = PEER_NKEYS * PEER_NKEYS
PEER_DKEY = 256
PEER_TOPK = 16
PEER_CHUNK = 1024
N_EVEN = (DEPTH + 1) // 2
N_ODD = DEPTH // 2

kernel_name = 'hybrid_bidir_swa_hyena_rwkv7_peer_adaln'


def rmsnorm(x, g):
    xf = x.astype(jnp.float32)
    y = xf * lax.rsqrt(jnp.mean(xf * xf, axis=-1, keepdims=True) + NORM_EPS)
    return (y * g.astype(jnp.float32)).astype(x.dtype)


def modulate(h, shift, scale):
    return h * (1 + scale[:, None, :]) + shift[:, None, :]


def rope(x):
    L, d = x.shape[1], x.shape[-1]
    half = d // 2
    inv = ROPE_THETA ** (-jnp.arange(half, dtype=jnp.float32) * 2.0 / d)
    ang = jnp.arange(L, dtype=jnp.float32)[:, None] * inv[None, :]
    cos = jnp.cos(ang)[None, :, None, :]
    sin = jnp.sin(ang)[None, :, None, :]
    xf = x.astype(jnp.float32)
    x1, x2 = xf[..., :half], xf[..., half:]
    return jnp.concatenate([x1 * cos - x2 * sin, x2 * cos + x1 * sin], axis=-1).astype(x.dtype)


def banded_gqa(q, k, v, sinks):
    B, L = q.shape[0], q.shape[1]
    nb = L // BLOCK
    pad = ((0, 0), (BLOCK, BLOCK), (0, 0), (0, 0))
    kp = jnp.pad(k, pad).reshape(B, nb + 2, BLOCK, ATT_KV_HEADS, ATT_HEAD_DIM)
    vp = jnp.pad(v, pad).reshape(B, nb + 2, BLOCK, ATT_KV_HEADS, ATT_HEAD_DIM)
    kb = jnp.concatenate([kp[:, :-2], kp[:, 1:-1], kp[:, 2:]], axis=2)
    vb = jnp.concatenate([vp[:, :-2], vp[:, 1:-1], vp[:, 2:]], axis=2)
    qb = q.reshape(B, nb, BLOCK, ATT_KV_HEADS, ATT_GROUP, ATT_HEAD_DIM)
    s = jnp.einsum('bnqhgd,bnkhd->bnhgqk', qb, kb, preferred_element_type=jnp.float32) * (ATT_HEAD_DIM ** -0.5)
    blk = jnp.arange(nb)[:, None]
    qpos = blk * BLOCK + jnp.arange(BLOCK)[None, :]
    kpos = (blk - 1) * BLOCK + jnp.arange(3 * BLOCK)[None, :]
    valid = (jnp.abs(qpos[:, :, None] - kpos[:, None, :]) <= WINDOW) & (kpos[:, None, :] >= 0) & (kpos[:, None, :] < L)
    s = jnp.where(valid[None, :, None, None], s, -jnp.inf)
    sink = sinks.astype(jnp.float32).reshape(ATT_KV_HEADS, ATT_GROUP)[None, None, :, :, None, None]
    m = jnp.maximum(jnp.max(s, axis=-1, keepdims=True), sink)
    p = jnp.exp(s - m)
    p = p / (jnp.sum(p, axis=-1, keepdims=True) + jnp.exp(sink - m))
    o = jnp.einsum('bnhgqk,bnkhd->bnqhgd', p.astype(vb.dtype), vb)
    return o.reshape(B, L, Q_W)


def short_conv3(x, w, b):
    xp = jnp.pad(x, ((0, 0), (1, 1), (0, 0)))
    return xp[:, :-2] * w[0] + xp[:, 1:-1] * w[1] + xp[:, 2:] * w[2] + b


def hyena_filter(L, fw1, fb1, ff1, fw2, fb2, ff2, fw3):
    f32 = jnp.float32
    t = jnp.linspace(0.0, 1.0, L, dtype=f32)[:, None]
    w = 2.0 * math.pi * jnp.arange(L, dtype=f32) / L
    bands = jnp.linspace(1e-4, HY_BANDS - 1, HY_BANDS, dtype=f32)
    ang = w[:, None] * bands[None, :]
    z = jnp.concatenate([t, jnp.cos(ang), -jnp.sin(ang)], axis=-1)
    hf = jnp.sin(ff1.astype(f32) * (z @ fw1.astype(f32) + fb1.astype(f32)))
    hf = jnp.sin(ff2.astype(f32) * (hf @ fw2.astype(f32) + fb2.astype(f32)))
    hf = (hf @ fw3.astype(f32)).reshape(L, 2, HY_WIDTH)
    max_decay = math.log(HY_DECAY_TARGET) / HY_FAST_DECAY
    min_decay = math.log(HY_DECAY_TARGET) / HY_SLOW_DECAY
    deltas = jnp.abs(jnp.linspace(min_decay, max_decay, HY_WIDTH, dtype=f32))
    window = jnp.exp(-t * deltas[None, :]) + HY_MOD_SHIFT
    hf = hf * window[:, None, :]
    h_fwd, h_bwd = hf[:, 0], hf[:, 1]
    two_sided = jnp.concatenate([h_fwd[:1] + h_bwd[:1], h_fwd[1:], jnp.zeros((1, HY_WIDTH), f32), h_bwd[1:][::-1]], axis=0)
    return two_sided / jnp.sum(jnp.abs(two_sided), axis=0, keepdims=True)


def fft_long_conv(u, filt, bias):
    L = u.shape[1]
    uf = jnp.fft.rfft(u, n=2 * L, axis=1)
    hf = jnp.fft.rfft(filt, n=2 * L, axis=0)
    y = jnp.fft.irfft(uf * hf[None], n=2 * L, axis=1)[:, :L]
    return y + u * bias


def even_mixer(h, P, i):
    f32 = jnp.float32
    B, L, _ = h.shape
    z = h @ P['ev_w_in'][i]
    q = rope(z[..., :Q_W].reshape(B, L, ATT_Q_HEADS, ATT_HEAD_DIM))
    k = rope(z[..., Q_W:Q_W + KV_W].reshape(B, L, ATT_KV_HEADS, ATT_HEAD_DIM))
    v = z[..., Q_W + KV_W:Q_W + 2 * KV_W].reshape(B, L, ATT_KV_HEADS, ATT_HEAD_DIM)
    att = banded_gqa(q, k, v, P['ev_sinks'][i])
    u = short_conv3(z[..., Q_W + 2 * KV_W:], P['ev_conv_w'][i], P['ev_conv_b'][i]).astype(f32)
    x0, x1, hv = jnp.split(u, 3, axis=-1)
    filt = hyena_filter(L, P['ev_filt_w1'][i], P['ev_filt_b1'][i], P['ev_filt_f1'][i],
                        P['ev_filt_w2'][i], P['ev_filt_b2'][i], P['ev_filt_f2'][i], P['ev_filt_w3'][i])
    y_hy = x0 * fft_long_conv(hv * x1, filt, P['ev_hy_bias'][i].astype(f32))
    mixed = jnp.concatenate([att, y_hy.astype(att.dtype)], axis=-1)
    return mixed @ P['ev_w_out'][i]


def wkv7_scan(r, w, k, v, a, b, reverse):
    B, L, H, S = r.shape
    def step(state, inp):
        r_t, w_t, k_t, v_t, a_t, b_t = inp
        sa = jnp.einsum('bhij,bhj->bhi', state, a_t)
        state = state * w_t[:, :, None, :] + sa[..., None] * b_t[:, :, None, :] + v_t[..., None] * k_t[:, :, None, :]
        return state, jnp.einsum('bhij,bhj->bhi', state, r_t)
    xs = tuple(jnp.moveaxis(t, 1, 0) for t in (r, w, k, v, a, b))
    _, y = lax.scan(step, jnp.zeros((B, H, S, S), jnp.float32), xs, reverse=reverse)
    return jnp.moveaxis(y, 0, 1)


def rwkv_mixer(h, P, j):
    f32 = jnp.float32
    B, L, D = h.shape
    H, S = RW_HEADS, RW_HEAD
    hp = jnp.pad(h, ((0, 0), (1, 1), (0, 0)))
    xx = 0.5 * (hp[:, :-2] + hp[:, 2:]) - h
    lerp = P['od_lerp'][j]
    xr, xw, xk, xv, xa, xg = (h + xx * lerp[n] for n in range(6))
    r = (xr @ P['od_w_r'][j]).astype(f32).reshape(B, L, H, S)
    k = (xk @ P['od_w_k'][j]).astype(f32)
    v = (xv @ P['od_w_v'][j]).astype(f32).reshape(B, L, H, S)
    g = jax.nn.sigmoid(xg @ P['od_g1'][j]) @ P['od_g2'][j]
    kk = (k * P['od_k_k'][j].astype(f32)).reshape(B, L, H, S)
    kk = kk / jnp.maximum(jnp.sqrt(jnp.sum(kk * kk, axis=-1, keepdims=True)), 1e-12)
    r_k = P['od_r_k'][j].astype(f32)
    k_a = P['od_k_a'][j].astype(f32)
    outs = []
    bonuses = []
    for d, rev in ((0, False), (1, True)):
        w_log = -jax.nn.softplus(-(P['od_w0'][j][d] + jnp.tanh(xw @ P['od_w1'][j][d]) @ P['od_w2'][j][d]).astype(f32)) - 0.5
        decay = jnp.exp(-jnp.exp(w_log)).reshape(B, L, H, S)
        a = jax.nn.sigmoid((P['od_a0'][j][d] + (xa @ P['od_a1'][j][d]) @ P['od_a2'][j][d]).astype(f32))
        kd = (k * (1 + (a - 1) * k_a)).reshape(B, L, H, S)
        a4 = a.reshape(B, L, H, S)
        outs.append(wkv7_scan(r, decay, kd, v, -kk, kk * a4, rev))
        bonuses.append(jnp.sum(r * kd * r_k, axis=-1, keepdims=True) * v)
    y = outs[0] + outs[1]
    mu = jnp.mean(y, axis=-1, keepdims=True)
    var = jnp.mean(jnp.square(y - mu), axis=-1, keepdims=True)
    yn = ((y - mu) * lax.rsqrt(var + RW_GN_EPS)).reshape(B, L, D)
    yn = yn * P['od_gn_g'][j].astype(f32) + P['od_gn_b'][j].astype(f32) + (bonuses[0] + bonuses[1]).reshape(B, L, D)
    return (yn.astype(h.dtype) * g) @ P['od_w_o'][j]


def peer(h, P, l):
    f32 = jnp.float32
    B, L, D = h.shape
    T = B * L
    C = math.gcd(T, PEER_CHUNK)
    w_q = P['pk_w_q'][l]
    sub = P['pk_sub_keys'][l].astype(f32)
    u_tab = P['pk_u'][l]
    v_tab = P['pk_v'][l]
    def block(xc):
        q = (xc @ w_q).astype(f32).reshape(C, PEER_HEADS, 2, PEER_DKEY // 2)
        s = jnp.einsum('thpe,hpne->thpn', q, sub)
        sv, si = lax.top_k(s, PEER_TOPK)
        cand = (sv[:, :, 0, :, None] + sv[:, :, 1, None, :]).reshape(C, PEER_HEADS, PEER_TOPK * PEER_TOPK)
        cidx = (si[:, :, 0, :, None] * PEER_NKEYS + si[:, :, 1, None, :]).reshape(C, PEER_HEADS, PEER_TOPK * PEER_TOPK)
        top_s, top_i = lax.top_k(cand, PEER_TOPK)
        expert = jnp.take_along_axis(cidx, top_i, axis=-1)
        gw = jax.nn.softmax(top_s, axis=-1)
        gate = jnp.zeros((C, PEER_EXPERTS), f32).at[jnp.arange(C)[:, None, None], expert].add(gw)
        act = jax.nn.gelu(jnp.einsum('td,nd->tn', xc, u_tab).astype(f32), approximate=False)
        return (gate * act).astype(xc.dtype) @ v_tab
    y = lax.map(block, h.reshape(T // C, C, D))
    return y.reshape(B, L, D)


def trunk(x, c, P):
    for l in range(DEPTH):
        mod = jax.nn.silu(c) @ P['ada_w'][l] + P['ada_b'][l]
        sh1, sc1, gt1, sh2, sc2, gt2 = jnp.split(mod, 6, axis=-1)
        h = modulate(rmsnorm(x, P['norm_g'][l, 0]), sh1, sc1)
        if l % 2 == 0:
            m = even_mixer(h, P, l // 2)
        else:
            m = rwkv_mixer(h, P, l // 2)
        x = x + gt1[:, None, :] * m
        h = modulate(rmsnorm(x, P['norm_g'][l, 1]), sh2, sc2)
        x = x + gt2[:, None, :] * peer(h, P, l)
    return rmsnorm(x, P['final_g'])


def setup_inputs(seed: int = 0) -> dict:
    key = jax.random.key(seed)
    keys = iter(jax.random.split(key, 48))
    f32 = jnp.float32
    D = D_MODEL
    def nrm(shape, scale):
        return jax.random.normal(next(keys), shape, f32) * scale
    inp = {}
    inp['x_prompt'] = nrm((BATCH, SEQ, D), 1.0)
    inp['x_sample'] = nrm((DEC_BATCH, DEC_SEQ, D), 1.0)
    inp['c_prompt'] = nrm((BATCH, D), 1.0)
    inp['c_sample'] = nrm((DEC_BATCH, D), 1.0)
    inp['ada_w'] = nrm((DEPTH, D, 6 * D), 0.5 * D ** -0.5)
    inp['ada_b'] = nrm((DEPTH, 6 * D), 0.02)
    inp['norm_g'] = 1.0 + nrm((DEPTH, 2, D), 0.02)
    inp['final_g'] = 1.0 + nrm((D,), 0.02)
    inp['ev_w_in'] = nrm((N_EVEN, D, IN_W_EVEN), D ** -0.5)
    inp['ev_sinks'] = nrm((N_EVEN, ATT_Q_HEADS), 0.5)
    inp['ev_conv_w'] = nrm((N_EVEN, HY_SHORT, 3 * HY_WIDTH), HY_SHORT ** -0.5)
    inp['ev_conv_b'] = nrm((N_EVEN, 3 * HY_WIDTH), 0.02)
    inp['ev_filt_w1'] = nrm((N_EVEN, HY_EMB, HY_FFN), HY_EMB ** -0.5)
    inp['ev_filt_b1'] = nrm((N_EVEN, HY_FFN), 0.02)
    inp['ev_filt_f1'] = 1.0 + nrm((N_EVEN, HY_FFN), 0.02)
    inp['ev_filt_w2'] = nrm((N_EVEN, HY_FFN, HY_FFN), HY_FFN ** -0.5)
    inp['ev_filt_b2'] = nrm((N_EVEN, HY_FFN), 0.02)
    inp['ev_filt_f2'] = 1.0 + nrm((N_EVEN, HY_FFN), 0.02)
    inp['ev_filt_w3'] = nrm((N_EVEN, HY_FFN, 2 * HY_WIDTH), HY_FFN ** -0.5)
    inp['ev_hy_bias'] = nrm((N_EVEN, HY_WIDTH), 0.5)
    inp['ev_w_out'] = nrm((N_EVEN, D, D), D ** -0.5)
    inp['od_lerp'] = jax.random.uniform(next(keys), (N_ODD, 6, D), f32)
    inp['od_w_r'] = nrm((N_ODD, D, D), D ** -0.5)
    inp['od_w_k'] = nrm((N_ODD, D, D), D ** -0.5)
    inp['od_w_v'] = nrm((N_ODD, D, D), D ** -0.5)
    inp['od_w_o'] = nrm((N_ODD, D, D), D ** -0.5)
    inp['od_w0'] = jnp.linspace(-6.0, -1.0, D, dtype=f32)[None, None, :] + nrm((N_ODD, 2, D), 0.1)
    inp['od_w1'] = nrm((N_ODD, 2, D, RW_DECAY_LORA), D ** -0.5)
    inp['od_w2'] = nrm((N_ODD, 2, RW_DECAY_LORA, D), 0.1 * RW_DECAY_LORA ** -0.5)
    inp['od_a0'] = nrm((N_ODD, 2, D), 0.1)
    inp['od_a1'] = nrm((N_ODD, 2, D, RW_AAA_LORA), D ** -0.5)
    inp['od_a2'] = nrm((N_ODD, 2, RW_AAA_LORA, D), 0.1 * RW_AAA_LORA ** -0.5)
    inp['od_g1'] = nrm((N_ODD, D, RW_GATE_LORA), D ** -0.5)
    inp['od_g2'] = nrm((N_ODD, RW_GATE_LORA, D), RW_GATE_LORA ** -0.5)
    inp['od_k_k'] = 0.85 + nrm((N_ODD, D), 0.02)
    inp['od_k_a'] = 1.0 + nrm((N_ODD, D), 0.02)
    inp['od_r_k'] = nrm((N_ODD, RW_HEADS, RW_HEAD), 0.1)
    inp['od_gn_g'] = 1.0 + nrm((N_ODD, D), 0.02)
    inp['od_gn_b'] = nrm((N_ODD, D), 0.02)
    inp['pk_w_q'] = nrm((DEPTH, D, PEER_HEADS * PEER_DKEY), D ** -0.5)
    inp['pk_sub_keys'] = nrm((DEPTH, PEER_HEADS, 2, PEER_NKEYS, PEER_DKEY // 2), (PEER_DKEY // 2) ** -0.5)
    inp['pk_u'] = nrm((DEPTH, PEER_EXPERTS, D), D ** -0.5)
    inp['pk_v'] = nrm((DEPTH, PEER_EXPERTS, D), 1.0)
    return inp


def reference(x_prompt, x_sample, c_prompt, c_sample, ada_w, ada_b, norm_g, final_g,
              ev_w_in, ev_sinks, ev_conv_w, ev_conv_b, ev_filt_w1, ev_filt_b1, ev_filt_f1,
              ev_filt_w2, ev_filt_b2, ev_filt_f2, ev_filt_w3, ev_hy_bias, ev_w_out,
              od_lerp, od_w_r, od_w_k, od_w_v, od_w_o, od_w0, od_w1, od_w2, od_a0, od_a1, od_a2,
              od_g1, od_g2, od_k_k, od_k_a, od_r_k, od_gn_g, od_gn_b,
              pk_w_q, pk_sub_keys, pk_u, pk_v):
    P = dict(ada_w=ada_w, ada_b=ada_b, norm_g=norm_g, final_g=final_g,
             ev_w_in=ev_w_in, ev_sinks=ev_sinks, ev_conv_w=ev_conv_w, ev_conv_b=ev_conv_b,
             ev_filt_w1=ev_filt_w1, ev_filt_b1=ev_filt_b1, ev_filt_f1=ev_filt_f1,
             ev_filt_w2=ev_filt_w2, ev_filt_b2=ev_filt_b2, ev_filt_f2=ev_filt_f2,
             ev_filt_w3=ev_filt_w3, ev_hy_bias=ev_hy_bias, ev_w_out=ev_w_out,
             od_lerp=od_lerp, od_w_r=od_w_r, od_w_k=od_w_k, od_w_v=od_w_v, od_w_o=od_w_o,
             od_w0=od_w0, od_w1=od_w1, od_w2=od_w2, od_a0=od_a0, od_a1=od_a1, od_a2=od_a2,
             od_g1=od_g1, od_g2=od_g2, od_k_k=od_k_k, od_k_a=od_k_a, od_r_k=od_r_k,
             od_gn_g=od_gn_g, od_gn_b=od_gn_b,
             pk_w_q=pk_w_q, pk_sub_keys=pk_sub_keys, pk_u=pk_u, pk_v=pk_v)
    y_prompt = trunk(x_prompt, c_prompt, P)
    y_sample = trunk(x_sample, c_sample, P)
    return (y_prompt, y_sample)
```

```python
import functools
import math

import jax
import jax.numpy as jnp
import numpy as np
from jax import lax
from jax.experimental import pallas as pl
from jax.experimental.pallas import tpu as pltpu

F32 = jnp.float32
BF16 = jnp.bfloat16

LANES = 128
SUBLANES = 8
VMEM_LIMIT_BYTES = 56 * 1024 * 1024

NORM_EPS = 1e-6
ATT_HEAD_DIM = 128
ATT_GROUP = 4
WINDOW = 128
ROPE_THETA = 10000.0
HY_BANDS = 16
HY_DECAY_TARGET = 1e-2
HY_FAST_DECAY = 0.3
HY_SLOW_DECAY = 1.5
HY_MOD_SHIFT = 0.05
RW_HEAD = 64
RW_GN_EPS = 64e-5
RW_CHUNK = 64
PEER_NKEYS = 128
PEER_TOPK = 16
DFT_N2 = 128


def _cparams(*sem):
    return pltpu.CompilerParams(dimension_semantics=sem, vmem_limit_bytes=VMEM_LIMIT_BYTES)


def _tile(n, target, mult):
    t = min(n, target)
    t -= t % mult
    while t >= mult:
        if n % t == 0:
            return t
        t -= mult
    return n


def _bdot(a, b):
    return jnp.dot(a.astype(BF16), b.astype(BF16), preferred_element_type=F32)


def _bdot_nt(a, b):
    return lax.dot_general(a.astype(BF16), b.astype(BF16), (((1,), (1,)), ((), ())), preferred_element_type=F32)


def _hdot(a, b):
    return jnp.dot(a.astype(F32), b.astype(F32), preferred_element_type=F32, precision=lax.Precision.HIGHEST)


def _silu(x):
    return x * jax.nn.sigmoid(x)


def _log_decay(z):
    return -jnp.exp(-jax.nn.softplus(-z) - 0.5)


_ACTS = {None: None, "silu": _silu, "tanh": jnp.tanh, "sigmoid": jax.nn.sigmoid, "log_decay": _log_decay}


def _mm_kernel(*refs, in_act, out_act, has_bias, has_res):
    x_ref, w_ref = refs[0], refs[1]
    pos = 2
    bias_ref = res_ref = gate_ref = None
    if has_bias:
        bias_ref = refs[pos]
        pos += 1
    if has_res:
        res_ref, gate_ref = refs[pos], refs[pos + 1]
        pos += 2
    o_ref = refs[pos]
    x = x_ref[...]
    if in_act is not None:
        x = _ACTS[in_act](x.astype(F32))
    acc = _bdot(x, w_ref[...])
    if has_bias:
        acc = acc + bias_ref[...]
    if out_act is not None:
        acc = _ACTS[out_act](acc)
    if has_res:
        acc = res_ref[...] + gate_ref[...] * acc
    o_ref[...] = acc.astype(o_ref.dtype)


def matmul(x, w, *, bias=None, in_act=None, out_act=None, res=None, gate=None, rows_per_gate=None,
           out_dtype=F32, tm=1024, tn=512, name="matmul"):
    m, k = x.shape
    k2, n = w.shape
    assert k == k2
    has_res = res is not None
    tm = _tile(rows_per_gate if has_res else m, tm, SUBLANES)
    tn = _tile(n, tn, LANES)
    assert m % tm == 0
    in_specs = [pl.BlockSpec((tm, k), lambda j, i: (i, 0)), pl.BlockSpec((k, tn), lambda j, i: (0, j))]
    args = [x, w]
    if bias is not None:
        in_specs.append(pl.BlockSpec((1, tn), lambda j, i: (0, j)))
        args.append(bias)
    if has_res:
        rpg = rows_per_gate // tm
        in_specs.append(pl.BlockSpec((tm, tn), lambda j, i: (i, j)))
        in_specs.append(pl.BlockSpec((None, 1, tn), lambda j, i: (i // rpg, 0, j)))
        args += [res, gate]
    return pl.pallas_call(
        functools.partial(_mm_kernel, in_act=in_act, out_act=out_act, has_bias=bias is not None, has_res=has_res),
        out_shape=jax.ShapeDtypeStruct((m, n), out_dtype),
        grid=(n // tn, m // tm),
        in_specs=in_specs,
        out_specs=pl.BlockSpec((tm, tn), lambda j, i: (i, j)),
        compiler_params=_cparams("parallel", "parallel"),
        name=name,
    )(*args)


def _norm_kernel(*refs, has_res, has_mod, emit_x):
    pos = 0
    x = refs[pos][...]
    pos += 1
    if has_res:
        x = x + refs[pos + 1][...] * refs[pos][...]
        pos += 2
    g = refs[pos][...]
    pos += 1
    y = x * lax.rsqrt(jnp.mean(x * x, axis=-1, keepdims=True) + NORM_EPS) * g
    if has_mod:
        y = y * (1.0 + refs[pos + 1][...]) + refs[pos][...]
        pos += 2
    if emit_x:
        refs[pos][...] = x
        pos += 1
    refs[pos][...] = y.astype(refs[pos].dtype)


def norm_mod(x, g, *, shift=None, scale=None, res=None, gate=None, emit_x=False, out_dtype=BF16, tl=256):
    b, l, d = x.shape
    tl = _tile(l, tl, SUBLANES)
    row = pl.BlockSpec((None, tl, d), lambda bi, i: (bi, i, 0))
    vec = pl.BlockSpec((None, 1, d), lambda bi, i: (bi, 0, 0))
    in_specs, args = [row], [x]
    if res is not None:
        in_specs += [row, vec]
        args += [res, gate]
    in_specs.append(pl.BlockSpec((1, d), lambda bi, i: (0, 0)))
    args.append(g)
    if shift is not None:
        in_specs += [vec, vec]
        args += [shift, scale]
    out_shape = [jax.ShapeDtypeStruct((b, l, d), out_dtype)]
    out_specs = [row]
    if emit_x:
        out_shape.insert(0, jax.ShapeDtypeStruct((b, l, d), F32))
        out_specs.insert(0, row)
    outs = pl.pallas_call(
        functools.partial(_norm_kernel, has_res=res is not None, has_mod=shift is not None, emit_x=emit_x),
        out_shape=out_shape,
        grid=(b, l // tl),
        in_specs=in_specs,
        out_specs=out_specs,
        compiler_params=_cparams("parallel", "parallel"),
        name="norm_mod",
    )(*args)
    return outs if emit_x else outs[0]


def _rope(x, cos2, sin2):
    return x * cos2 + pltpu.roll(x, ATT_HEAD_DIM // 2, axis=1) * sin2


def _attn_kernel(sink_ref, q_ref, kp_ref, kc_ref, kn_ref, vp_ref, vc_ref, vn_ref,
                 cq_ref, sq_ref, cp_ref, sp_ref, cn_ref, sn_ref, o_ref, *, seq_len):
    n = pl.program_id(1)
    h = pl.program_id(2)
    blk = WINDOW
    k3 = jnp.concatenate([_rope(kp_ref[...], cp_ref[...], sp_ref[...]),
                          _rope(kc_ref[...], cq_ref[...], sq_ref[...]),
                          _rope(kn_ref[...], cn_ref[...], sn_ref[...])], axis=0).astype(BF16)
    v3 = jnp.concatenate([vp_ref[...], vc_ref[...], vn_ref[...]], axis=0).astype(BF16)
    qpos = n * blk + lax.broadcasted_iota(jnp.int32, (blk, 3 * blk), 0)
    kpos = (n - 1) * blk + lax.broadcasted_iota(jnp.int32, (blk, 3 * blk), 1)
    valid = (jnp.abs(qpos - kpos) <= WINDOW) & (kpos >= 0) & (kpos < seq_len)
    for g in range(ATT_GROUP):
        q = _rope(q_ref[:, g * ATT_HEAD_DIM:(g + 1) * ATT_HEAD_DIM], cq_ref[...], sq_ref[...])
        s = _bdot_nt(q, k3) * (ATT_HEAD_DIM ** -0.5)
        s = jnp.where(valid, s, -jnp.inf)
        sink = sink_ref[h * ATT_GROUP + g]
        m = jnp.maximum(jnp.max(s, axis=-1, keepdims=True), sink)
        p = jnp.exp(s - m)
        p = p / (jnp.sum(p, axis=-1, keepdims=True) + jnp.exp(sink - m))
        o_ref[:, g * ATT_HEAD_DIM:(g + 1) * ATT_HEAD_DIM] = _bdot(p, v3).astype(o_ref.dtype)


def banded_attention(z, sinks, q_w, kv_w):
    b, l, _ = z.shape
    blk = WINDOW
    nb = l // blk
    hd = ATT_HEAD_DIM
    kvh = kv_w // hd
    gw = ATT_GROUP * hd
    half = hd // 2
    inv = ROPE_THETA ** (-jnp.arange(half, dtype=F32) * 2.0 / hd)
    ang = jnp.arange(l, dtype=F32)[:, None] * inv[None, :]
    cos2 = jnp.concatenate([jnp.cos(ang), jnp.cos(ang)], axis=-1)
    sin2 = jnp.concatenate([-jnp.sin(ang), jnp.sin(ang)], axis=-1)
    kcol, vcol = q_w // hd, (q_w + kv_w) // hd

    def prev(i):
        return jnp.maximum(i - 1, 0)

    def nxt(i):
        return jnp.minimum(i + 1, nb - 1)

    def kv_spec(col0, which):
        return pl.BlockSpec((None, blk, hd), lambda bi, i, h: (bi, which(i), col0 + h))

    def tab_spec(which):
        return pl.BlockSpec((blk, hd), lambda bi, i, h: (which(i), 0))

    same = lambda i: i
    return pl.pallas_call(
        functools.partial(_attn_kernel, seq_len=l),
        out_shape=jax.ShapeDtypeStruct((b, l, q_w), BF16),
        grid=(b, nb, kvh),
        in_specs=[pl.BlockSpec(memory_space=pltpu.SMEM),
                  pl.BlockSpec((None, blk, gw), lambda bi, i, h: (bi, i, h)),
                  kv_spec(kcol, prev), kv_spec(kcol, same), kv_spec(kcol, nxt),
                  kv_spec(vcol, prev), kv_spec(vcol, same), kv_spec(vcol, nxt),
                  tab_spec(same), tab_spec(same), tab_spec(prev), tab_spec(prev), tab_spec(nxt), tab_spec(nxt)],
        out_specs=pl.BlockSpec((None, blk, gw), lambda bi, i, h: (bi, i, h)),
        compiler_params=_cparams("parallel", "parallel", "parallel"),
        name="banded_attention",
    )(sinks, z, z, z, z, z, z, z, cos2, sin2, cos2, sin2, cos2, sin2)


def _shift_rows(x, prev_row, next_row):
    tl = x.shape[0]
    row = lax.broadcasted_iota(jnp.int32, x.shape, 0)
    up = jnp.where(row == 0, prev_row, pltpu.roll(x, 1, axis=0))
    dn = jnp.where(row == tl - 1, next_row, pltpu.roll(x, tl - 1, axis=0))
    return up, dn


def _halo_rows(prev_ref, next_ref):
    i = pl.program_id(1)
    last = pl.num_programs(1) - 1
    prev_row = jnp.where(i == 0, 0.0, prev_ref[SUBLANES - 1:SUBLANES, :])
    next_row = jnp.where(i == last, 0.0, next_ref[0:1, :])
    return prev_row, next_row


def _hy_pro_kernel(*refs):
    groups = [refs[3 * g:3 * g + 3] for g in range(3)]
    cw_refs = refs[9:12]
    cb_refs = refs[12:15]
    bias_ref = refs[15]
    w_ref, x0_ref, t2_ref = refs[16:19]
    u = []
    for (c_ref, p_ref, n_ref), cw_ref, cb_ref in zip(groups, cw_refs, cb_refs):
        x = c_ref[...]
        prev_row, next_row = _halo_rows(p_ref, n_ref)
        up, dn = _shift_rows(x, prev_row, next_row)
        u.append(up * cw_ref[0:1, :] + x * cw_ref[1:2, :] + dn * cw_ref[2:3, :] + cb_ref[...])
    x0, x1, hv = u
    w = hv * x1
    w_ref[...] = w.astype(w_ref.dtype)
    x0_ref[...] = x0.astype(x0_ref.dtype)
    t2_ref[...] = (x0 * (w * bias_ref[...])).astype(t2_ref.dtype)


def hyena_prologue(z, conv_w, conv_b, hy_bias, col0, c):
    b, l, _ = z.shape
    tl = _tile(l, 256, SUBLANES)
    tc = _tile(c, 512, LANES)
    nh = l // SUBLANES
    tps = tl // SUBLANES
    in_specs, args = [], []
    for g in range(3):
        cb0 = (col0 + g * c) // tc
        in_specs += [
            pl.BlockSpec((None, tl, tc), lambda bi, i, j, cb0=cb0: (bi, i, cb0 + j)),
            pl.BlockSpec((None, SUBLANES, tc), lambda bi, i, j, cb0=cb0: (bi, jnp.maximum(i * tps - 1, 0), cb0 + j)),
            pl.BlockSpec((None, SUBLANES, tc), lambda bi, i, j, cb0=cb0: (bi, jnp.minimum((i + 1) * tps, nh - 1), cb0 + j)),
        ]
        args += [z, z, z]
    for g in range(3):
        in_specs.append(pl.BlockSpec((3, tc), lambda bi, i, j, g=g: (0, g * (c // tc) + j)))
        args.append(conv_w)
    for g in range(3):
        in_specs.append(pl.BlockSpec((1, tc), lambda bi, i, j, g=g: (0, g * (c // tc) + j)))
        args.append(conv_b)
    in_specs.append(pl.BlockSpec((1, tc), lambda bi, i, j: (0, j)))
    args.append(hy_bias)
    out_spec = pl.BlockSpec((None, tl, tc), lambda bi, i, j: (bi, i, j))
    return pl.pallas_call(
        _hy_pro_kernel,
        out_shape=[jax.ShapeDtypeStruct((b, l, c), BF16)] * 3,
        grid=(b, l // tl, c // tc),
        in_specs=in_specs,
        out_specs=[out_spec] * 3,
        compiler_params=_cparams("parallel", "parallel", "parallel"),
        name="hyena_prologue",
    )(*args)


def _hy_filter_kernel(z_ref, w1_ref, b1_ref, f1_ref, w2_ref, b2_ref, f2_ref, w3_ref, dl_ref,
                      h_ref, asum_ref, row0_ref, *, seq_len):
    i = pl.program_id(0)
    tl = z_ref.shape[0]
    c = dl_ref.shape[1]
    h1 = jnp.sin(f1_ref[...] * (_hdot(z_ref[...], w1_ref[...]) + b1_ref[...]))
    h2 = jnp.sin(f2_ref[...] * (_hdot(h1, w2_ref[...]) + b2_ref[...]))
    h3 = _hdot(h2, w3_ref[...])
    row = i * tl + lax.broadcasted_iota(jnp.int32, (tl, 1), 0)
    t = row.astype(F32) * (1.0 / (seq_len - 1))
    window = jnp.exp(-t * dl_ref[...]) + HY_MOD_SHIFT
    h3 = h3 * jnp.concatenate([window, window], axis=1)
    h_ref[...] = h3

    @pl.when(i == 0)
    def _():
        asum_ref[...] = jnp.zeros_like(asum_ref)
        row0_ref[...] = h3[0:1, :]

    asum_ref[...] += jnp.sum(jnp.where(row == 0, 0.0, jnp.abs(h3)), axis=0, keepdims=True)


def hyena_filter(l, w1, b1, f1, w2, b2, f2, w3, c):
    t = jnp.linspace(0.0, 1.0, l, dtype=F32)[:, None]
    w = 2.0 * math.pi * jnp.arange(l, dtype=F32) / l
    bands = jnp.linspace(1e-4, HY_BANDS - 1, HY_BANDS, dtype=F32)
    ang = w[:, None] * bands[None, :]
    z = jnp.concatenate([t, jnp.cos(ang), -jnp.sin(ang)], axis=-1)
    emb = z.shape[1]
    emb_pad = -(-emb // SUBLANES) * SUBLANES
    z = jnp.pad(z, ((0, 0), (0, emb_pad - emb)))
    w1 = jnp.pad(w1, ((0, emb_pad - emb), (0, 0)))
    max_decay = math.log(HY_DECAY_TARGET) / HY_FAST_DECAY
    min_decay = math.log(HY_DECAY_TARGET) / HY_SLOW_DECAY
    deltas = jnp.abs(jnp.linspace(min_decay, max_decay, c, dtype=F32))[None, :]
    ffn = w2.shape[0]
    tl = _tile(l, 512, SUBLANES)
    full = lambda shape: pl.BlockSpec(shape, lambda i: (0, 0))
    h, asum, row0 = pl.pallas_call(
        functools.partial(_hy_filter_kernel, seq_len=l),
        out_shape=[jax.ShapeDtypeStruct((l, 2 * c), F32), jax.ShapeDtypeStruct((1, 2 * c), F32),
                   jax.ShapeDtypeStruct((1, 2 * c), F32)],
        grid=(l // tl,),
        in_specs=[pl.BlockSpec((tl, emb_pad), lambda i: (i, 0)), full((emb_pad, ffn)), full((1, ffn)), full((1, ffn)),
                  full((ffn, ffn)), full((1, ffn)), full((1, ffn)), full((ffn, 2 * c)), full((1, c))],
        out_specs=[pl.BlockSpec((tl, 2 * c), lambda i: (i, 0)), full((1, 2 * c)), full((1, 2 * c))],
        compiler_params=_cparams("arbitrary"),
        name="hyena_filter",
    )(z, w1, b1, f1, w2, b2, f2, w3, deltas)
    norm = asum[:, :c] + asum[:, c:] + jnp.abs(row0[:, :c] + row0[:, c:])
    return h, norm


def _dft_tables(n1, n2):
    n = n1 * n2
    n1h = n1 // 2
    th1 = 2.0 * np.pi * np.outer(np.arange(n1), np.arange(n1h)) / n1
    f1 = np.concatenate([np.cos(th1), -np.sin(th1)], axis=0)
    th1i = 2.0 * np.pi * np.outer(np.arange(n1h), np.arange(n1)) / n1
    f1inv_re, f1inv_im = np.cos(th1i), -np.sin(th1i)
    k1 = np.arange(n1)[:, None, None]
    k2 = np.arange(n2)[None, :, None]
    m2 = np.arange(n2)[None, None, :]
    th = 2.0 * np.pi * (m2 * k2 / n2 + m2 * k1 / n)
    gr, gi = np.cos(th), -np.sin(th)
    g = np.concatenate([np.concatenate([gr, -gi], axis=2), np.concatenate([gi, gr], axis=2)], axis=1)
    grt, git = np.swapaxes(gr, 1, 2), np.swapaxes(gi, 1, 2)
    ginv = np.concatenate([np.concatenate([grt, git], axis=2), np.concatenate([-git, grt], axis=2)], axis=1)
    to = lambda a: jnp.asarray(a.astype(np.float32)).astype(BF16)
    return to(f1), (to(f1inv_re), to(f1inv_im)), to(g), to(ginv)


def _lmm_kernel(w_ref, x_ref, o_ref):
    o_ref[...] = _bdot(w_ref[...], x_ref[...]).astype(o_ref.dtype)


def left_matmul(w, x, out_dtype, tc):
    b, k, n = x.shape
    m = w.shape[0]
    tc = _tile(n, tc, LANES)
    return pl.pallas_call(
        _lmm_kernel,
        out_shape=jax.ShapeDtypeStruct((b, m, n), out_dtype),
        grid=(b, n // tc),
        in_specs=[pl.BlockSpec((m, k), lambda bi, j: (0, 0)), pl.BlockSpec((None, k, tc), lambda bi, j: (bi, 0, j))],
        out_specs=pl.BlockSpec((None, m, tc), lambda bi, j: (bi, 0, j)),
        compiler_params=_cparams("parallel", "parallel"),
        name="dft_stage_a",
    )(w, x)


def _dft_mid_fwd_kernel(g_ref, ar_ref, ai_ref, xr_ref, xi_ref):
    n2 = ar_ref.shape[0]
    x = _bdot(g_ref[...], jnp.concatenate([ar_ref[...], ai_ref[...]], axis=0))
    xr_ref[...] = x[:n2]
    xi_ref[...] = x[n2:]


def dft_mid_forward(g, a):
    _, _, n1, n2, c = a.shape
    tc = _tile(c, 512, LANES)
    a_spec = lambda ri: pl.BlockSpec((None, None, None, n2, tc), lambda k, j: (0, ri, k, 0, j))
    o_spec = pl.BlockSpec((None, n2, tc), lambda k, j: (k, 0, j))
    spec = jax.ShapeDtypeStruct((n1, n2, c), F32)
    return pl.pallas_call(
        _dft_mid_fwd_kernel,
        out_shape=[spec, spec],
        grid=(n1, c // tc),
        in_specs=[pl.BlockSpec((None, 2 * n2, 2 * n2), lambda k, j: (k, 0, 0)), a_spec(0), a_spec(1)],
        out_specs=[o_spec, o_spec],
        compiler_params=_cparams("parallel", "parallel"),
        name="dft_mid_forward",
    )(g, a, a)


def _dft_mid_kernel(g_ref, gi_ref, ar_ref, ai_ref, hfr_ref, hfi_ref, hbr_ref, hbi_ref, br_ref, bi_ref):
    n2 = ar_ref.shape[0]
    x = _bdot(g_ref[...], jnp.concatenate([ar_ref[...], ai_ref[...]], axis=0))
    xr, xi = x[:n2], x[n2:]
    hr = hfr_ref[...] + hbr_ref[...]
    hi = hfi_ref[...] - hbi_ref[...]
    y = jnp.concatenate([xr * hr - xi * hi, xr * hi + xi * hr], axis=0)
    bm = _bdot(gi_ref[...], y)
    br_ref[...] = bm[:n2].astype(br_ref.dtype)
    bi_ref[...] = bm[n2:].astype(bi_ref.dtype)


def dft_mid(g, ginv, a, hr, hi):
    b, _, n1, n2, c = a.shape
    tc = _tile(c, 512, LANES)
    nct = c // tc
    a_spec = lambda ri: pl.BlockSpec((None, None, None, n2, tc), lambda k, bi, j: (bi, ri, k, 0, j))
    h_spec = lambda off: pl.BlockSpec((None, n2, tc), lambda k, bi, j: (k, 0, off + j))
    g_spec = pl.BlockSpec((None, 2 * n2, 2 * n2), lambda k, bi, j: (k, 0, 0))
    o_spec = pl.BlockSpec((None, None, n2, tc), lambda k, bi, j: (bi, k, 0, j))
    out = jax.ShapeDtypeStruct((b, n1, n2, c), BF16)
    return pl.pallas_call(
        _dft_mid_kernel,
        out_shape=[out, out],
        grid=(n1, b, nct),
        in_specs=[g_spec, g_spec, a_spec(0), a_spec(1), h_spec(0), h_spec(0), h_spec(nct), h_spec(nct)],
        out_specs=[o_spec, o_spec],
        compiler_params=_cparams("parallel", "parallel", "parallel"),
        name="dft_mid",
    )(g, ginv, a, a, hr, hi, hr, hi)


def _dft_out_kernel(fr_ref, fi_ref, br_ref, bi_ref, x0_ref, t2_ref, sc_ref, o_ref):
    y = _bdot(fr_ref[...], br_ref[...]) + _bdot(fi_ref[...], bi_ref[...])
    o_ref[...] = (x0_ref[...].astype(F32) * (y * sc_ref[...]) + t2_ref[...].astype(F32)).astype(o_ref.dtype)


def dft_out(f1inv, br, bi, x0, t2, scale):
    b, n1, cols = br.shape
    n1h = f1inv[0].shape[0]
    tc = scale.shape[1]
    io = pl.BlockSpec((None, n1h, tc), lambda bi_, j: (bi_, 0, j))
    bspec = pl.BlockSpec((None, n1, tc), lambda bi_, j: (bi_, 0, j))
    fspec = pl.BlockSpec((n1h, n1), lambda bi_, j: (0, 0))
    return pl.pallas_call(
        _dft_out_kernel,
        out_shape=jax.ShapeDtypeStruct((b, n1h, cols), BF16),
        grid=(b, cols // tc),
        in_specs=[fspec, fspec, bspec, bspec, io, io, pl.BlockSpec((1, tc), lambda bi_, j: (0, 0))],
        out_specs=io,
        compiler_params=_cparams("parallel", "parallel"),
        name="dft_stage_a_inverse",
    )(f1inv[0], f1inv[1], br, bi, x0, t2, scale)


def hyena_long_conv(w16, x0, t2, hfilt, norm):
    b, l, c = w16.shape
    n2 = DFT_N2 if (2 * l) % DFT_N2 == 0 and 2 * l // DFT_N2 >= 2 else 2
    n1 = 2 * l // n2
    n1h = n1 // 2
    f1, f1inv, g, ginv = _dft_tables(n1, n2)
    ha = left_matmul(f1, hfilt.astype(BF16).reshape(1, n1h, n2 * 2 * c), BF16, 4096)
    hr, hi = dft_mid_forward(g, ha.reshape(1, 2, n1, n2, 2 * c))
    a = left_matmul(f1, w16.reshape(b, n1h, n2 * c), BF16, 4096)
    br, bi = dft_mid(g, ginv, a.reshape(b, 2, n1, n2, c), hr, hi)
    reps = max(1, 2048 // c)
    scale = jnp.tile(1.0 / (norm * (2.0 * l)), (1, reps))
    y = dft_out(f1inv, br.reshape(b, n1, n2 * c), bi.reshape(b, n1, n2 * c),
                x0.reshape(b, n1h, n2 * c), t2.reshape(b, n1h, n2 * c), scale)
    return y.reshape(b, l, c)


def even_mixer(h, z_w, p, i, q_w):
    b, l, d = h.shape
    c = d - q_w
    in_w = z_w.shape[1]
    kv_w = (in_w - q_w - 3 * c) // 2
    z = matmul(h.reshape(b * l, d), z_w, name="even_in_proj").reshape(b, l, in_w)
    att = banded_attention(z, p["ev_sinks"][i], q_w, kv_w)
    w16, x0, t2 = hyena_prologue(z, p["ev_conv_w"][i], p["ev_conv_b"][i][None], p["ev_hy_bias"][i][None],
                                 q_w + 2 * kv_w, c)
    hfilt, norm = hyena_filter(l, p["ev_filt_w1"][i], p["ev_filt_b1"][i][None], p["ev_filt_f1"][i][None],
                               p["ev_filt_w2"][i], p["ev_filt_b2"][i][None], p["ev_filt_f2"][i][None],
                               p["ev_filt_w3"][i], c)
    y_hy = hyena_long_conv(w16, x0, t2, hfilt, norm)
    return jnp.concatenate([att, y_hy], axis=-1).reshape(b * l, d)


def _rw_mix_kernel(h_ref, p_ref, n_ref, lerp_ref, *o_refs):
    h = h_ref[...]
    prev_row, next_row = _halo_rows(p_ref, n_ref)
    up, dn = _shift_rows(h, prev_row, next_row)
    xx = 0.5 * (up + dn) - h
    for n, o_ref in enumerate(o_refs):
        o_ref[...] = (h + xx * lerp_ref[n:n + 1, :]).astype(o_ref.dtype)


def rwkv_mix(h, lerp):
    b, l, d = h.shape
    n = lerp.shape[0]
    tl = _tile(l, 256, SUBLANES)
    tc = _tile(d, 512, LANES)
    nh = l // SUBLANES
    tps = tl // SUBLANES
    blk = pl.BlockSpec((None, tl, tc), lambda bi, i, j: (bi, i, j))
    return pl.pallas_call(
        _rw_mix_kernel,
        out_shape=[jax.ShapeDtypeStruct((b, l, d), BF16)] * n,
        grid=(b, l // tl, d // tc),
        in_specs=[blk,
                  pl.BlockSpec((None, SUBLANES, tc), lambda bi, i, j: (bi, jnp.maximum(i * tps - 1, 0), j)),
                  pl.BlockSpec((None, SUBLANES, tc), lambda bi, i, j: (bi, jnp.minimum((i + 1) * tps, nh - 1), j)),
                  pl.BlockSpec((n, tc), lambda bi, i, j: (0, j))],
        out_specs=[blk] * n,
        compiler_params=_cparams("parallel", "parallel", "parallel"),
        name="rwkv_mix",
    )(h, h, h, lerp)


def _head_sum(x, m0):
    s0 = jnp.sum(jnp.where(m0, x, 0.0), axis=-1, keepdims=True)
    s1 = jnp.sum(jnp.where(m0, 0.0, x), axis=-1, keepdims=True)
    return jnp.where(m0, s0, s1)


def _wkv_kernel(r_ref, k_ref, v_ref, lw_ref, a_ref, kk_ref, ka_ref, rk_ref, y_ref, bon_ref, s_ref, *, rev):
    @pl.when(pl.program_id(2) == 0)
    def _():
        s_ref[...] = jnp.zeros_like(s_ref)

    c = r_ref.shape[0]
    r, k, v, lw, a = r_ref[...], k_ref[...], v_ref[...], lw_ref[...], a_ref[...]
    m0 = lax.broadcasted_iota(jnp.int32, r.shape, 1) < RW_HEAD
    kk = k * kk_ref[...]
    kk = kk / jnp.maximum(jnp.sqrt(_head_sum(kk * kk, m0)), 1e-12)
    kd = k * (1.0 + (a - 1.0) * ka_ref[...])
    bvec = kk * a
    avec = -kk
    bon_ref[...] = _head_sum(r * kd * rk_ref[...], m0) * v

    row = lax.broadcasted_iota(jnp.int32, (c, c), 0)
    col = lax.broadcasted_iota(jnp.int32, (c, c), 1)
    tri = (col >= row) if rev else (col <= row)
    ci = _hdot(tri.astype(F32), lw)
    ctot = jnp.sum(lw, axis=0, keepdims=True)
    inv = jnp.exp(-ci)
    tail = jnp.exp(ctot - ci)

    def stack(x):
        return jnp.concatenate([jnp.where(m0, x, 0.0), jnp.where(m0, 0.0, x)], axis=0)

    aqs = stack(avec * jnp.exp(ci - lw))
    rqs = stack(r * jnp.exp(ci))
    kds = stack(kd * inv)
    bds = stack(bvec * inv)
    kcs = stack(kd * tail)
    bcs = stack(bvec * tail)
    vs = stack(v)

    c2 = 2 * c
    row2 = lax.broadcasted_iota(jnp.int32, (c2, c2), 0)
    col2 = lax.broadcasted_iota(jnp.int32, (c2, c2), 1)
    strict = (col2 > row2) if rev else (col2 < row2)
    incl = (col2 >= row2) if rev else (col2 <= row2)
    m_all = _bdot_nt(jnp.concatenate([aqs, rqs], axis=0), jnp.concatenate([bds, kds], axis=0))
    m_ab = jnp.where(strict, m_all[:c2, :c2], 0.0)
    m_ak = jnp.where(strict, m_all[:c2, c2:], 0.0)
    m_rb = jnp.where(incl, m_all[c2:, :c2], 0.0)
    m_rk = jnp.where(incl, m_all[c2:, c2:], 0.0)

    x = jnp.where(row2 == col2, 1.0, 0.0) + m_ab
    pw = m_ab
    for _ in range(int(math.log2(c)) - 1):
        pw = _bdot(pw, pw)
        x = x + _bdot(x, pw)

    s = s_ref[...]
    us = _bdot(x, _bdot_nt(aqs, s) + _bdot(m_ak, vs))
    ys = _bdot_nt(rqs, s) + _bdot(m_rk, vs) + _bdot(m_rb, us)
    y_ref[...] = ys[:c] + ys[c:]
    upd = lax.dot_general(jnp.concatenate([vs, us], axis=0).astype(BF16),
                          jnp.concatenate([kcs, bcs], axis=0).astype(BF16),
                          (((0,), (0,)), ((), ())), preferred_element_type=F32)
    s_ref[...] = s * jnp.exp(ctot) + upd


def wkv_scan(r, k, v, lw, a, k_k, k_a, r_k, rev):
    b, l, d = r.shape
    c = RW_CHUNK
    nch = l // c
    pair = 2 * RW_HEAD
    cidx = (lambda ci: nch - 1 - ci) if rev else (lambda ci: ci)
    seq = pl.BlockSpec((None, c, pair), lambda bi, hp, ci: (bi, cidx(ci), hp))
    par = pl.BlockSpec((1, pair), lambda bi, hp, ci: (0, hp))
    out = jax.ShapeDtypeStruct((b, l, d), F32)
    return pl.pallas_call(
        functools.partial(_wkv_kernel, rev=rev),
        out_shape=[out, out],
        grid=(b, d // pair, nch),
        in_specs=[seq] * 5 + [par] * 3,
        out_specs=[seq, seq],
        scratch_shapes=[pltpu.VMEM((pair, pair), F32)],
        compiler_params=_cparams("parallel", "parallel", "arbitrary"),
        name="wkv_scan_bwd" if rev else "wkv_scan_fwd",
    )(r, k, v, lw, a, k_k, k_a, r_k)


def _rw_post_kernel(yf_ref, yb_ref, bf_ref, bb_ref, g_ref, gg_ref, gb_ref, o_ref):
    y = yf_ref[...] + yb_ref[...]
    m0 = lax.broadcasted_iota(jnp.int32, y.shape, 1) < RW_HEAD
    mu = _head_sum(y, m0) * (1.0 / RW_HEAD)
    dlt = y - mu
    var = _head_sum(dlt * dlt, m0) * (1.0 / RW_HEAD)
    yn = dlt * lax.rsqrt(var + RW_GN_EPS) * gg_ref[...] + gb_ref[...] + bf_ref[...] + bb_ref[...]
    o_ref[...] = (yn * g_ref[...]).astype(o_ref.dtype)


def rwkv_post(yf, yb, bf, bb, g, gn_g, gn_b):
    b, l, d = yf.shape
    pair = 2 * RW_HEAD
    tl = _tile(l, 1024, SUBLANES)
    blk = pl.BlockSpec((None, tl, pair), lambda bi, i, j: (bi, i, j))
    par = pl.BlockSpec((1, pair), lambda bi, i, j: (0, j))
    return pl.pallas_call(
        _rw_post_kernel,
        out_shape=jax.ShapeDtypeStruct((b, l, d), BF16),
        grid=(b, l // tl, d // pair),
        in_specs=[blk] * 5 + [par, par],
        out_specs=blk,
        compiler_params=_cparams("parallel", "parallel", "parallel"),
        name="rwkv_post",
    )(yf, yb, bf, bb, g, gn_g, gn_b)


def rwkv_mixer(h, p, j):
    b, l, d = h.shape
    t = b * l
    xr, xw, xk, xv, xa, xg = (x.reshape(t, d) for x in rwkv_mix(h, p["od_lerp"][j]))
    r = matmul(xr, p["od_w_r"][j], name="rwkv_r")
    k = matmul(xk, p["od_w_k"][j], name="rwkv_k")
    v = matmul(xv, p["od_w_v"][j], name="rwkv_v")
    g = matmul(matmul(xg, p["od_g1"][j], name="rwkv_g1"), p["od_g2"][j], in_act="sigmoid", name="rwkv_g2")
    w1 = jnp.concatenate([p["od_w1"][j][0], p["od_w1"][j][1]], axis=1)
    a1 = jnp.concatenate([p["od_a1"][j][0], p["od_a1"][j][1]], axis=1)
    tw = matmul(xw, w1, name="rwkv_w1")
    ta = matmul(xa, a1, name="rwkv_a1")
    nl = tw.shape[1] // 2
    na = ta.shape[1] // 2
    shp = (b, l, d)
    ys, bons = [], []
    for di, rev in ((0, False), (1, True)):
        lw = matmul(tw[:, di * nl:(di + 1) * nl], p["od_w2"][j][di], bias=p["od_w0"][j][di][None], in_act="tanh",
                    out_act="log_decay", name="rwkv_w2")
        a = matmul(ta[:, di * na:(di + 1) * na], p["od_a2"][j][di], bias=p["od_a0"][j][di][None],
                   out_act="sigmoid", name="rwkv_a2")
        y, bon = wkv_scan(r.reshape(shp), k.reshape(shp), v.reshape(shp), lw.reshape(shp), a.reshape(shp),
                          p["od_k_k"][j][None], p["od_k_a"][j][None], p["od_r_k"][j].reshape(1, d), rev)
        ys.append(y)
        bons.append(bon)
    out = rwkv_post(ys[0], ys[1], bons[0], bons[1], g.reshape(shp), p["od_gn_g"][j][None], p["od_gn_b"][j][None])
    return out.reshape(t, d)


def _top_values(s, k):
    rows = lax.broadcasted_iota(jnp.int32, (k, s.shape[1]), 0)

    def body(i, carry):
        s, vals = carry
        m = jnp.max(s, axis=0, keepdims=True)
        return jnp.where(s >= m, -jnp.inf, s), jnp.where(rows == i, m, vals)

    return lax.fori_loop(0, k, body, (s, jnp.full((k, s.shape[1]), -jnp.inf, F32)))[1]


def _peer_topk_kernel(q_ref, sub_ref, s_ref, e_ref, tau_ref):
    k = PEER_TOPK
    for h in range(tau_ref.shape[0]):
        s1 = _bdot(sub_ref[2 * h], q_ref[2 * h])
        s2 = _bdot(sub_ref[2 * h + 1], q_ref[2 * h + 1])
        v1 = _top_values(s1, k)
        v2 = _top_values(s2, k)
        cand = jnp.concatenate([v1[i:i + 1, :] + v2 for i in range(k)], axis=0)
        top = _top_values(cand, k)
        z = jnp.sum(jnp.exp(top - top[0:1, :]), axis=0, keepdims=True)
        s_ref[2 * h] = s1
        s_ref[2 * h + 1] = s2
        e_ref[2 * h] = jnp.exp(s1 - v1[0:1, :])
        e_ref[2 * h + 1] = jnp.exp(s2 - v2[0:1, :]) / z
        tau_ref[h:h + 1, :] = top[k - 1:k, :]


def peer_topk(q_t, sub):
    hp, dk, t = q_t.shape
    nk = sub.shape[1]
    tm = _tile(t, 512, LANES)
    blk = pl.BlockSpec((hp, nk, tm), lambda i: (0, 0, i))
    big = jax.ShapeDtypeStruct((hp, nk, t), F32)
    return pl.pallas_call(
        _peer_topk_kernel,
        out_shape=[big, big, jax.ShapeDtypeStruct((hp // 2, t), F32)],
        grid=(t // tm,),
        in_specs=[pl.BlockSpec((hp, dk, tm), lambda i: (0, 0, i)), pl.BlockSpec((hp, nk, dk), lambda i: (0, 0, 0))],
        out_specs=[blk, blk, pl.BlockSpec((hp // 2, tm), lambda i: (0, i))],
        compiler_params=_cparams("parallel"),
        name="peer_topk",
    )(q_t, sub)


def _peer_main_kernel(u_ref, x_ref, v_ref, s_ref, e_ref, tau_ref, o_ref, *, na):
    e = pl.program_id(1)

    @pl.when(e == 0)
    def _():
        o_ref[...] = jnp.zeros_like(o_ref)

    nk = s_ref.shape[1]
    act = _bdot(u_ref[...], x_ref[...])
    act = 0.5 * act * (1.0 + lax.erf(act * (2.0 ** -0.5)))
    pieces = []
    for al in range(na):
        a = e * na + al
        gate = jnp.zeros((nk, act.shape[1]), F32)
        for h in range(tau_ref.shape[0]):
            s1 = s_ref[2 * h, pl.ds(a, 1), :]
            e1 = e_ref[2 * h, pl.ds(a, 1), :]
            hit = (s1 + s_ref[2 * h + 1]) >= tau_ref[h:h + 1, :]
            gate = gate + jnp.where(hit, e1 * e_ref[2 * h + 1], 0.0)
        pieces.append(gate * act[al * nk:(al + 1) * nk, :])
    o_ref[...] += _bdot(v_ref[...], jnp.concatenate(pieces, axis=0))


def peer_main(u, x_t, v_t, s, e, tau):
    ne, d = u.shape
    t = x_t.shape[1]
    nk = s.shape[1]
    tm = _tile(t, 512, LANES)
    na = 2
    te = na * nk
    tok = lambda shape: pl.BlockSpec(shape, lambda i, j: (0,) * (len(shape) - 1) + (i,))
    return pl.pallas_call(
        functools.partial(_peer_main_kernel, na=na),
        out_shape=jax.ShapeDtypeStruct((d, t), F32),
        grid=(t // tm, ne // te),
        in_specs=[pl.BlockSpec((te, d), lambda i, j: (j, 0)), tok((d, tm)), pl.BlockSpec((d, te), lambda i, j: (0, j)),
                  tok(s.shape[:2] + (tm,)), tok(e.shape[:2] + (tm,)), tok((tau.shape[0], tm))],
        out_specs=tok((d, tm)),
        compiler_params=_cparams("parallel", "arbitrary"),
        name="peer_main",
    )(u, x_t, v_t, s, e, tau)


def peer(h, wq_t, sub, u, v_t):
    x_t = h.T
    q_t = matmul(wq_t, x_t, name="peer_query")
    hp, nk, dk = sub.shape
    s, e, tau = peer_topk(q_t.reshape(hp, dk, -1), sub)
    return peer_main(u, x_t, v_t, s, e, tau).T


def _prepare(w):
    p = dict(w)
    for name in ("ev_w_in", "ev_w_out", "od_w_r", "od_w_k", "od_w_v", "od_w_o", "od_w1", "od_w2", "od_a1", "od_a2",
                 "od_g1", "od_g2", "pk_u"):
        p[name] = w[name].astype(BF16)
    p["pk_v_t"] = jnp.swapaxes(w["pk_v"].astype(BF16), 1, 2)
    p["pk_wq_t"] = jnp.swapaxes(w["pk_w_q"].astype(BF16), 1, 2)
    sk = w["pk_sub_keys"]
    p["pk_sub"] = sk.astype(BF16).reshape(sk.shape[0], sk.shape[1] * sk.shape[2], sk.shape[3], sk.shape[4])
    return p


def _trunk(x, mods, p):
    b, l, d = x.shape
    t = b * l
    depth = p["ada_w"].shape[0]
    q_w = p["ev_sinks"].shape[1] * ATT_HEAD_DIM
    y = gt2 = None
    for layer in range(depth):
        sh1, sc1, gt1, sh2, sc2, gt2_next = mods[layer]
        even = layer % 2 == 0
        g1 = p["norm_g"][layer, 0][None]
        kw = dict(shift=sh1, scale=sc1, out_dtype=BF16 if even else F32)
        if y is None:
            h = norm_mod(x, g1, **kw)
        else:
            x, h = norm_mod(x, g1, res=y, gate=gt2, emit_x=True, **kw)
        if even:
            pre, w_out = even_mixer(h, p["ev_w_in"][layer // 2], p, layer // 2, q_w), p["ev_w_out"][layer // 2]
        else:
            pre, w_out = rwkv_mixer(h, p, layer // 2), p["od_w_o"][layer // 2]
        x = matmul(pre, w_out, res=x.reshape(t, d), gate=gt1, rows_per_gate=l, name="mixer_out_proj").reshape(b, l, d)
        h2 = norm_mod(x, p["norm_g"][layer, 1][None], shift=sh2, scale=sc2)
        y = peer(h2.reshape(t, d), p["pk_wq_t"][layer], p["pk_sub"][layer], p["pk_u"][layer], p["pk_v_t"][layer])
        y = y.reshape(b, l, d)
        gt2 = gt2_next
    return norm_mod(x, p["final_g"][None], res=y, gate=gt2, out_dtype=F32)


def kernel(x_prompt, x_sample, c_prompt, c_sample, ada_w, ada_b, norm_g, final_g, ev_w_in, ev_sinks, ev_conv_w, ev_conv_b, ev_filt_w1, ev_filt_b1, ev_filt_f1, ev_filt_w2, ev_filt_b2, ev_filt_f2, ev_filt_w3, ev_hy_bias, ev_w_out, od_lerp, od_w_r, od_w_k, od_w_v, od_w_o, od_w0, od_w1, od_w2, od_a0, od_a1, od_a2, od_g1, od_g2, od_k_k, od_k_a, od_r_k, od_gn_g, od_gn_b, pk_w_q, pk_sub_keys, pk_u, pk_v):
    p = _prepare(dict(
        ada_w=ada_w, ada_b=ada_b, norm_g=norm_g, final_g=final_g, ev_w_in=ev_w_in, ev_sinks=ev_sinks,
        ev_conv_w=ev_conv_w, ev_conv_b=ev_conv_b, ev_filt_w1=ev_filt_w1, ev_filt_b1=ev_filt_b1, ev_filt_f1=ev_filt_f1,
        ev_filt_w2=ev_filt_w2, ev_filt_b2=ev_filt_b2, ev_filt_f2=ev_filt_f2, ev_filt_w3=ev_filt_w3,
        ev_hy_bias=ev_hy_bias, ev_w_out=ev_w_out, od_lerp=od_lerp, od_w_r=od_w_r, od_w_k=od_w_k, od_w_v=od_w_v,
        od_w_o=od_w_o, od_w0=od_w0, od_w1=od_w1, od_w2=od_w2, od_a0=od_a0, od_a1=od_a1, od_a2=od_a2, od_g1=od_g1,
        od_g2=od_g2, od_k_k=od_k_k, od_k_a=od_k_a, od_r_k=od_r_k, od_gn_g=od_gn_g, od_gn_b=od_gn_b,
        pk_w_q=pk_w_q, pk_sub_keys=pk_sub_keys, pk_u=pk_u, pk_v=pk_v))
    depth, d = ada_w.shape[0], ada_w.shape[1]
    bp, bs = c_prompt.shape[0], c_sample.shape[0]
    rows = -(-(bp + bs) // SUBLANES) * SUBLANES
    c_all = jnp.pad(jnp.concatenate([c_prompt, c_sample], axis=0), ((0, rows - bp - bs), (0, 0)))
    mods_p, mods_s = [], []
    for layer in range(depth):
        mod = matmul(c_all, ada_w[layer], bias=ada_b[layer][None], in_act="silu", name="adaln")
        parts = jnp.split(mod, 6, axis=-1)
        mods_p.append([m[:bp, None, :] for m in parts])
        mods_s.append([m[bp:bp + bs, None, :] for m in parts])
    return _trunk(x_prompt, mods_p, p), _trunk(x_sample, mods_s, p)
```

```python
import functools
import math

import jax
import jax.numpy as jnp
import numpy as np
from jax import lax
from jax.experimental import pallas as pl
from jax.experimental.pallas import tpu as pltpu

F32 = jnp.float32
BF16 = jnp.bfloat16

LANES = 128
SUBLANES = 8
VMEM_LIMIT_BYTES = 56 * 1024 * 1024

NORM_EPS = 1e-6
ATT_HEAD_DIM = 128
ATT_GROUP = 4
WINDOW = 128
ROPE_THETA = 10000.0
HY_BANDS = 16
HY_DECAY_TARGET = 1e-2
HY_FAST_DECAY = 0.3
HY_SLOW_DECAY = 1.5
HY_MOD_SHIFT = 0.05
RW_HEAD = 64
RW_GN_EPS = 64e-5
RW_CHUNK = 64
WKV_PAIRS_PER_STEP = 8
PEER_NKEYS = 128
PEER_TOPK = 16
PEER_KEYS_PER_SUBTILE = 2
DFT_N2 = 128


def _cparams(*sem):
    return pltpu.CompilerParams(dimension_semantics=sem, vmem_limit_bytes=VMEM_LIMIT_BYTES)


def _tile(n, target, mult):
    t = min(n, target)
    t -= t % mult
    while t >= mult:
        if n % t == 0:
            return t
        t -= mult
    return n


def _bdot(a, b):
    return jnp.dot(a.astype(BF16), b.astype(BF16), preferred_element_type=F32)


def _bdot_nt(a, b):
    return lax.dot_general(a.astype(BF16), b.astype(BF16), (((1,), (1,)), ((), ())), preferred_element_type=F32)


def _hdot(a, b):
    return jnp.dot(a.astype(F32), b.astype(F32), preferred_element_type=F32, precision=lax.Precision.HIGHEST)


def _silu(x):
    return x * jax.nn.sigmoid(x)


def _log_decay(z):
    return -jnp.exp(-jax.nn.softplus(-z) - 0.5)


_ACTS = {None: None, "silu": _silu, "tanh": jnp.tanh, "sigmoid": jax.nn.sigmoid, "log_decay": _log_decay}


def _mm_kernel(*refs, in_act, out_act, has_bias, has_res):
    x_ref, w_ref = refs[0], refs[1]
    pos = 2
    bias_ref = res_ref = gate_ref = None
    if has_bias:
        bias_ref = refs[pos]
        pos += 1
    if has_res:
        res_ref, gate_ref = refs[pos], refs[pos + 1]
        pos += 2
    o_ref = refs[pos]
    x = x_ref[...]
    if in_act is not None:
        x = _ACTS[in_act](x.astype(F32))
    acc = _bdot(x, w_ref[...])
    if has_bias:
        acc = acc + bias_ref[...]
    if out_act is not None:
        acc = _ACTS[out_act](acc)
    if has_res:
        acc = res_ref[...] + gate_ref[...] * acc
    o_ref[...] = acc.astype(o_ref.dtype)


def matmul(x, w, *, bias=None, in_act=None, out_act=None, res=None, gate=None, rows_per_gate=None,
           out_dtype=F32, tm=1024, tn=512, name="matmul"):
    m, k = x.shape
    k2, n = w.shape
    assert k == k2
    has_res = res is not None
    tm = _tile(rows_per_gate if has_res else m, tm, SUBLANES)
    tn = _tile(n, tn, LANES)
    assert m % tm == 0
    in_specs = [pl.BlockSpec((tm, k), lambda j, i: (i, 0)), pl.BlockSpec((k, tn), lambda j, i: (0, j))]
    args = [x, w]
    if bias is not None:
        in_specs.append(pl.BlockSpec((1, tn), lambda j, i: (0, j)))
        args.append(bias)
    if has_res:
        rpg = rows_per_gate // tm
        in_specs.append(pl.BlockSpec((tm, tn), lambda j, i: (i, j)))
        in_specs.append(pl.BlockSpec((None, 1, tn), lambda j, i: (i // rpg, 0, j)))
        args += [res, gate]
    return pl.pallas_call(
        functools.partial(_mm_kernel, in_act=in_act, out_act=out_act, has_bias=bias is not None, has_res=has_res),
        out_shape=jax.ShapeDtypeStruct((m, n), out_dtype),
        grid=(n // tn, m // tm),
        in_specs=in_specs,
        out_specs=pl.BlockSpec((tm, tn), lambda j, i: (i, j)),
        compiler_params=_cparams("parallel", "parallel"),
        name=name,
    )(*args)


def _norm_kernel(*refs, has_res, has_mod, emit_x, transposed):
    pos = 0
    x = refs[pos][...]
    pos += 1
    if has_res:
        x = x + refs[pos + 1][...] * refs[pos][...]
        pos += 2
    g = refs[pos][...]
    pos += 1
    y = x * lax.rsqrt(jnp.mean(x * x, axis=-1, keepdims=True) + NORM_EPS) * g
    if has_mod:
        y = y * (1.0 + refs[pos + 1][...]) + refs[pos][...]
        pos += 2
    if emit_x:
        refs[pos][...] = x
        pos += 1
    if transposed:
        y = y.T
    refs[pos][...] = y.astype(refs[pos].dtype)


def norm_mod(x, g, *, shift=None, scale=None, res=None, gate=None, emit_x=False, out_dtype=BF16, tl=256,
             transposed=False):
    b, l, d = x.shape
    tl = _tile(l, tl, LANES if transposed else SUBLANES)
    nl = l // tl
    row = pl.BlockSpec((None, tl, d), lambda bi, i: (bi, i, 0))
    vec = pl.BlockSpec((None, 1, d), lambda bi, i: (bi, 0, 0))
    in_specs, args = [row], [x]
    if res is not None:
        in_specs += [row, vec]
        args += [res, gate]
    in_specs.append(pl.BlockSpec((1, d), lambda bi, i: (0, 0)))
    args.append(g)
    if shift is not None:
        in_specs += [vec, vec]
        args += [shift, scale]
    if transposed:
        out_shape = [jax.ShapeDtypeStruct((d, b * l), out_dtype)]
        out_specs = [pl.BlockSpec((d, tl), lambda bi, i: (0, bi * nl + i))]
    else:
        out_shape = [jax.ShapeDtypeStruct((b, l, d), out_dtype)]
        out_specs = [row]
    if emit_x:
        out_shape.insert(0, jax.ShapeDtypeStruct((b, l, d), F32))
        out_specs.insert(0, row)
    outs = pl.pallas_call(
        functools.partial(_norm_kernel, has_res=res is not None, has_mod=shift is not None, emit_x=emit_x,
                          transposed=transposed),
        out_shape=out_shape,
        grid=(b, nl),
        in_specs=in_specs,
        out_specs=out_specs,
        compiler_params=_cparams("parallel", "parallel"),
        name="norm_mod",
    )(*args)
    return outs if emit_x else outs[0]


def _rope(x, cos2, sin2):
    return x * cos2 + pltpu.roll(x, ATT_HEAD_DIM // 2, axis=1) * sin2


def _attn_kernel(sink_ref, q_ref, kp_ref, kc_ref, kn_ref, vp_ref, vc_ref, vn_ref,
                 cq_ref, sq_ref, cp_ref, sp_ref, cn_ref, sn_ref, o_ref, *, seq_len):
    n = pl.program_id(1)
    h = pl.program_id(2)
    blk = WINDOW
    k3 = jnp.concatenate([_rope(kp_ref[...], cp_ref[...], sp_ref[...]),
                          _rope(kc_ref[...], cq_ref[...], sq_ref[...]),
                          _rope(kn_ref[...], cn_ref[...], sn_ref[...])], axis=0).astype(BF16)
    v3 = jnp.concatenate([vp_ref[...], vc_ref[...], vn_ref[...]], axis=0).astype(BF16)
    qpos = n * blk + lax.broadcasted_iota(jnp.int32, (blk, 3 * blk), 0)
    kpos = (n - 1) * blk + lax.broadcasted_iota(jnp.int32, (blk, 3 * blk), 1)
    valid = (jnp.abs(qpos - kpos) <= WINDOW) & (kpos >= 0) & (kpos < seq_len)
    for g in range(ATT_GROUP):
        q = _rope(q_ref[:, g * ATT_HEAD_DIM:(g + 1) * ATT_HEAD_DIM], cq_ref[...], sq_ref[...])
        s = _bdot_nt(q, k3) * (ATT_HEAD_DIM ** -0.5)
        s = jnp.where(valid, s, -jnp.inf)
        sink = sink_ref[h * ATT_GROUP + g]
        m = jnp.maximum(jnp.max(s, axis=-1, keepdims=True), sink)
        p = jnp.exp(s - m)
        p = p / (jnp.sum(p, axis=-1, keepdims=True) + jnp.exp(sink - m))
        o_ref[:, g * ATT_HEAD_DIM:(g + 1) * ATT_HEAD_DIM] = _bdot(p, v3).astype(o_ref.dtype)


def banded_attention(z, sinks, q_w, kv_w):
    b, l, _ = z.shape
    blk = WINDOW
    nb = l // blk
    hd = ATT_HEAD_DIM
    kvh = kv_w // hd
    gw = ATT_GROUP * hd
    half = hd // 2
    inv = ROPE_THETA ** (-jnp.arange(half, dtype=F32) * 2.0 / hd)
    ang = jnp.arange(l, dtype=F32)[:, None] * inv[None, :]
    cos2 = jnp.concatenate([jnp.cos(ang), jnp.cos(ang)], axis=-1)
    sin2 = jnp.concatenate([-jnp.sin(ang), jnp.sin(ang)], axis=-1)
    kcol, vcol = q_w // hd, (q_w + kv_w) // hd

    def prev(i):
        return jnp.maximum(i - 1, 0)

    def nxt(i):
        return jnp.minimum(i + 1, nb - 1)

    def kv_spec(col0, which):
        return pl.BlockSpec((None, blk, hd), lambda bi, i, h: (bi, which(i), col0 + h))

    def tab_spec(which):
        return pl.BlockSpec((blk, hd), lambda bi, i, h: (which(i), 0))

    same = lambda i: i
    return pl.pallas_call(
        functools.partial(_attn_kernel, seq_len=l),
        out_shape=jax.ShapeDtypeStruct((b, l, q_w), BF16),
        grid=(b, nb, kvh),
        in_specs=[pl.BlockSpec(memory_space=pltpu.SMEM),
                  pl.BlockSpec((None, blk, gw), lambda bi, i, h: (bi, i, h)),
                  kv_spec(kcol, prev), kv_spec(kcol, same), kv_spec(kcol, nxt),
                  kv_spec(vcol, prev), kv_spec(vcol, same), kv_spec(vcol, nxt),
                  tab_spec(same), tab_spec(same), tab_spec(prev), tab_spec(prev), tab_spec(nxt), tab_spec(nxt)],
        out_specs=pl.BlockSpec((None, blk, gw), lambda bi, i, h: (bi, i, h)),
        compiler_params=_cparams("parallel", "parallel", "parallel"),
        name="banded_attention",
    )(sinks, z, z, z, z, z, z, z, cos2, sin2, cos2, sin2, cos2, sin2)


def _shift_rows(x, prev_row, next_row):
    tl = x.shape[0]
    row = lax.broadcasted_iota(jnp.int32, x.shape, 0)
    up = jnp.where(row == 0, prev_row, pltpu.roll(x, 1, axis=0))
    dn = jnp.where(row == tl - 1, next_row, pltpu.roll(x, tl - 1, axis=0))
    return up, dn


def _halo_rows(prev_ref, next_ref):
    i = pl.program_id(1)
    last = pl.num_programs(1) - 1
    prev_row = jnp.where(i == 0, 0.0, prev_ref[SUBLANES - 1:SUBLANES, :])
    next_row = jnp.where(i == last, 0.0, next_ref[0:1, :])
    return prev_row, next_row


def _hy_pro_kernel(*refs):
    groups = [refs[3 * g:3 * g + 3] for g in range(3)]
    cw_refs = refs[9:12]
    cb_refs = refs[12:15]
    bias_ref = refs[15]
    w_ref, x0_ref, t2_ref = refs[16:19]
    u = []
    for (c_ref, p_ref, n_ref), cw_ref, cb_ref in zip(groups, cw_refs, cb_refs):
        x = c_ref[...]
        prev_row, next_row = _halo_rows(p_ref, n_ref)
        up, dn = _shift_rows(x, prev_row, next_row)
        u.append(up * cw_ref[0:1, :] + x * cw_ref[1:2, :] + dn * cw_ref[2:3, :] + cb_ref[...])
    x0, x1, hv = u
    w = hv * x1
    w_ref[...] = w.astype(w_ref.dtype)
    x0_ref[...] = x0.astype(x0_ref.dtype)
    t2_ref[...] = (x0 * (w * bias_ref[...])).astype(t2_ref.dtype)


def hyena_prologue(z, conv_w, conv_b, hy_bias, col0, c):
    b, l, _ = z.shape
    tl = _tile(l, 256, SUBLANES)
    tc = _tile(c, 512, LANES)
    nh = l // SUBLANES
    tps = tl // SUBLANES
    in_specs, args = [], []
    for g in range(3):
        cb0 = (col0 + g * c) // tc
        in_specs += [
            pl.BlockSpec((None, tl, tc), lambda bi, i, j, cb0=cb0: (bi, i, cb0 + j)),
            pl.BlockSpec((None, SUBLANES, tc), lambda bi, i, j, cb0=cb0: (bi, jnp.maximum(i * tps - 1, 0), cb0 + j)),
            pl.BlockSpec((None, SUBLANES, tc), lambda bi, i, j, cb0=cb0: (bi, jnp.minimum((i + 1) * tps, nh - 1), cb0 + j)),
        ]
        args += [z, z, z]
    for g in range(3):
        in_specs.append(pl.BlockSpec((3, tc), lambda bi, i, j, g=g: (0, g * (c // tc) + j)))
        args.append(conv_w)
    for g in range(3):
        in_specs.append(pl.BlockSpec((1, tc), lambda bi, i, j, g=g: (0, g * (c // tc) + j)))
        args.append(conv_b)
    in_specs.append(pl.BlockSpec((1, tc), lambda bi, i, j: (0, j)))
    args.append(hy_bias)
    out_spec = pl.BlockSpec((None, tl, tc), lambda bi, i, j: (bi, i, j))
    return pl.pallas_call(
        _hy_pro_kernel,
        out_shape=[jax.ShapeDtypeStruct((b, l, c), BF16)] * 3,
        grid=(b, l // tl, c // tc),
        in_specs=in_specs,
        out_specs=[out_spec] * 3,
        compiler_params=_cparams("parallel", "parallel", "parallel"),
        name="hyena_prologue",
    )(*args)


def _hy_filter_kernel(z_ref, w1_ref, b1_ref, f1_ref, w2_ref, b2_ref, f2_ref, w3_ref, dl_ref,
                      h_ref, asum_ref, row0_ref, *, seq_len):
    i = pl.program_id(0)
    tl = z_ref.shape[0]
    c = dl_ref.shape[1]
    h1 = jnp.sin(f1_ref[...] * (_hdot(z_ref[...], w1_ref[...]) + b1_ref[...]))
    h2 = jnp.sin(f2_ref[...] * (_hdot(h1, w2_ref[...]) + b2_ref[...]))
    h3 = _hdot(h2, w3_ref[...])
    row = i * tl + lax.broadcasted_iota(jnp.int32, (tl, 1), 0)
    t = row.astype(F32) * (1.0 / (seq_len - 1))
    window = jnp.exp(-t * dl_ref[...]) + HY_MOD_SHIFT
    h3 = h3 * jnp.concatenate([window, window], axis=1)
    h_ref[...] = h3

    @pl.when(i == 0)
    def _():
        asum_ref[...] = jnp.zeros_like(asum_ref)
        row0_ref[...] = h3[0:1, :]

    asum_ref[...] += jnp.sum(jnp.where(row == 0, 0.0, jnp.abs(h3)), axis=0, keepdims=True)


def hyena_filter(l, w1, b1, f1, w2, b2, f2, w3, c):
    t = jnp.linspace(0.0, 1.0, l, dtype=F32)[:, None]
    w = 2.0 * math.pi * jnp.arange(l, dtype=F32) / l
    bands = jnp.linspace(1e-4, HY_BANDS - 1, HY_BANDS, dtype=F32)
    ang = w[:, None] * bands[None, :]
    z = jnp.concatenate([t, jnp.cos(ang), -jnp.sin(ang)], axis=-1)
    emb = z.shape[1]
    emb_pad = -(-emb // SUBLANES) * SUBLANES
    z = jnp.pad(z, ((0, 0), (0, emb_pad - emb)))
    w1 = jnp.pad(w1, ((0, emb_pad - emb), (0, 0)))
    max_decay = math.log(HY_DECAY_TARGET) / HY_FAST_DECAY
    min_decay = math.log(HY_DECAY_TARGET) / HY_SLOW_DECAY
    deltas = jnp.abs(jnp.linspace(min_decay, max_decay, c, dtype=F32))[None, :]
    ffn = w2.shape[0]
    tl = _tile(l, 512, SUBLANES)
    full = lambda shape: pl.BlockSpec(shape, lambda i: (0, 0))
    h, asum, row0 = pl.pallas_call(
        functools.partial(_hy_filter_kernel, seq_len=l),
        out_shape=[jax.ShapeDtypeStruct((l, 2 * c), F32), jax.ShapeDtypeStruct((1, 2 * c), F32),
                   jax.ShapeDtypeStruct((1, 2 * c), F32)],
        grid=(l // tl,),
        in_specs=[pl.BlockSpec((tl, emb_pad), lambda i: (i, 0)), full((emb_pad, ffn)), full((1, ffn)), full((1, ffn)),
                  full((ffn, ffn)), full((1, ffn)), full((1, ffn)), full((ffn, 2 * c)), full((1, c))],
        out_specs=[pl.BlockSpec((tl, 2 * c), lambda i: (i, 0)), full((1, 2 * c)), full((1, 2 * c))],
        compiler_params=_cparams("arbitrary"),
        name="hyena_filter",
    )(z, w1, b1, f1, w2, b2, f2, w3, deltas)
    norm = asum[:, :c] + asum[:, c:] + jnp.abs(row0[:, :c] + row0[:, c:])
    return h, norm


def _dft_tables(n1, n2):
    n = n1 * n2
    n1h = n1 // 2
    th1 = 2.0 * np.pi * np.outer(np.arange(n1), np.arange(n1h)) / n1
    f1 = np.concatenate([np.cos(th1), -np.sin(th1)], axis=0)
    th1i = 2.0 * np.pi * np.outer(np.arange(n1h), np.arange(n1)) / n1
    f1inv_re, f1inv_im = np.cos(th1i), -np.sin(th1i)
    k1 = np.arange(n1)[:, None, None]
    k2 = np.arange(n2)[None, :, None]
    m2 = np.arange(n2)[None, None, :]
    th = 2.0 * np.pi * (m2 * k2 / n2 + m2 * k1 / n)
    gr, gi = np.cos(th), -np.sin(th)
    g = np.concatenate([np.concatenate([gr, -gi], axis=2), np.concatenate([gi, gr], axis=2)], axis=1)
    grt, git = np.swapaxes(gr, 1, 2), np.swapaxes(gi, 1, 2)
    ginv = np.concatenate([np.concatenate([grt, git], axis=2), np.concatenate([-git, grt], axis=2)], axis=1)
    to = lambda a: jnp.asarray(a.astype(np.float32)).astype(BF16)
    return to(f1), (to(f1inv_re), to(f1inv_im)), to(g), to(ginv)


def _lmm_kernel(w_ref, x_ref, o_ref):
    o_ref[...] = _bdot(w_ref[...], x_ref[...]).astype(o_ref.dtype)


def left_matmul(w, x, out_dtype, tc):
    b, k, n = x.shape
    m = w.shape[0]
    tc = _tile(n, tc, LANES)
    return pl.pallas_call(
        _lmm_kernel,
        out_shape=jax.ShapeDtypeStruct((b, m, n), out_dtype),
        grid=(b, n // tc),
        in_specs=[pl.BlockSpec((m, k), lambda bi, j: (0, 0)), pl.BlockSpec((None, k, tc), lambda bi, j: (bi, 0, j))],
        out_specs=pl.BlockSpec((None, m, tc), lambda bi, j: (bi, 0, j)),
        compiler_params=_cparams("parallel", "parallel"),
        name="dft_stage_a",
    )(w, x)


def _dft_mid_fwd_kernel(g_ref, fr_ref, fi_ref, br_ref, bi_ref, hr_ref, hi_ref):
    n2 = fr_ref.shape[0]
    xf = _bdot(g_ref[...], jnp.concatenate([fr_ref[...], fi_ref[...]], axis=0))
    xb = _bdot(g_ref[...], jnp.concatenate([br_ref[...], bi_ref[...]], axis=0))
    hr_ref[...] = xf[:n2] + xb[:n2]
    hi_ref[...] = xf[n2:] - xb[n2:]


def dft_mid_forward(g, a):
    _, _, n1, n2, c2 = a.shape
    c = c2 // 2
    tc = _tile(c, 512, LANES)
    nct = c // tc
    a_spec = lambda ri, off: pl.BlockSpec((None, None, None, n2, tc), lambda k, j: (0, ri, k, 0, off + j))
    o_spec = pl.BlockSpec((None, n2, tc), lambda k, j: (k, 0, j))
    spec = jax.ShapeDtypeStruct((n1, n2, c), F32)
    return pl.pallas_call(
        _dft_mid_fwd_kernel,
        out_shape=[spec, spec],
        grid=(n1, nct),
        in_specs=[pl.BlockSpec((None, 2 * n2, 2 * n2), lambda k, j: (k, 0, 0)),
                  a_spec(0, 0), a_spec(1, 0), a_spec(0, nct), a_spec(1, nct)],
        out_specs=[o_spec, o_spec],
        compiler_params=_cparams("parallel", "parallel"),
        name="dft_mid_forward",
    )(g, a, a, a, a)


def _dft_mid_kernel(g_ref, gi_ref, ar_ref, ai_ref, hr_ref, hi_ref, br_ref, bi_ref):
    n2 = ar_ref.shape[0]
    x = _bdot(g_ref[...], jnp.concatenate([ar_ref[...], ai_ref[...]], axis=0))
    xr, xi = x[:n2], x[n2:]
    hr, hi = hr_ref[...], hi_ref[...]
    y = jnp.concatenate([xr * hr - xi * hi, xr * hi + xi * hr], axis=0)
    bm = _bdot(gi_ref[...], y)
    br_ref[...] = bm[:n2].astype(br_ref.dtype)
    bi_ref[...] = bm[n2:].astype(bi_ref.dtype)


def dft_mid(g, ginv, a, hr, hi):
    b, _, n1, n2, c = a.shape
    tc = _tile(c, 512, LANES)
    a_spec = lambda ri: pl.BlockSpec((None, None, None, n2, tc), lambda k, bi, j: (bi, ri, k, 0, j))
    h_spec = pl.BlockSpec((None, n2, tc), lambda k, bi, j: (k, 0, j))
    g_spec = pl.BlockSpec((None, 2 * n2, 2 * n2), lambda k, bi, j: (k, 0, 0))
    o_spec = pl.BlockSpec((None, None, n2, tc), lambda k, bi, j: (bi, k, 0, j))
    out = jax.ShapeDtypeStruct((b, n1, n2, c), BF16)
    return pl.pallas_call(
        _dft_mid_kernel,
        out_shape=[out, out],
        grid=(n1, b, c // tc),
        in_specs=[g_spec, g_spec, a_spec(0), a_spec(1), h_spec, h_spec],
        out_specs=[o_spec, o_spec],
        compiler_params=_cparams("parallel", "parallel", "parallel"),
        name="dft_mid",
    )(g, ginv, a, a, hr, hi)


def _dft_out_kernel(fr_ref, fi_ref, br_ref, bi_ref, x0_ref, t2_ref, sc_ref, o_ref):
    y = _bdot(fr_ref[...], br_ref[...]) + _bdot(fi_ref[...], bi_ref[...])
    o_ref[...] = (x0_ref[...].astype(F32) * (y * sc_ref[...]) + t2_ref[...].astype(F32)).astype(o_ref.dtype)


def dft_out(f1inv, br, bi, x0, t2, scale):
    b, n1, cols = br.shape
    n1h = f1inv[0].shape[0]
    tc = scale.shape[1]
    io = pl.BlockSpec((None, n1h, tc), lambda bi_, j: (bi_, 0, j))
    bspec = pl.BlockSpec((None, n1, tc), lambda bi_, j: (bi_, 0, j))
    fspec = pl.BlockSpec((n1h, n1), lambda bi_, j: (0, 0))
    return pl.pallas_call(
        _dft_out_kernel,
        out_shape=jax.ShapeDtypeStruct((b, n1h, cols), BF16),
        grid=(b, cols // tc),
        in_specs=[fspec, fspec, bspec, bspec, io, io, pl.BlockSpec((1, tc), lambda bi_, j: (0, 0))],
        out_specs=io,
        compiler_params=_cparams("parallel", "parallel"),
        name="dft_stage_a_inverse",
    )(f1inv[0], f1inv[1], br, bi, x0, t2, scale)


def hyena_long_conv(w16, x0, t2, hfilt, norm):
    b, l, c = w16.shape
    n2 = DFT_N2 if (2 * l) % DFT_N2 == 0 and 2 * l // DFT_N2 >= 2 else 2
    n1 = 2 * l // n2
    n1h = n1 // 2
    f1, f1inv, g, ginv = _dft_tables(n1, n2)
    ha = left_matmul(f1, hfilt.astype(BF16).reshape(1, n1h, n2 * 2 * c), BF16, 4096)
    hr, hi = dft_mid_forward(g, ha.reshape(1, 2, n1, n2, 2 * c))
    a = left_matmul(f1, w16.reshape(b, n1h, n2 * c), BF16, 4096)
    br, bi = dft_mid(g, ginv, a.reshape(b, 2, n1, n2, c), hr, hi)
    reps = max(1, 2048 // c)
    scale = jnp.tile(1.0 / (norm * (2.0 * l)), (1, reps))
    y = dft_out(f1inv, br.reshape(b, n1, n2 * c), bi.reshape(b, n1, n2 * c),
                x0.reshape(b, n1h, n2 * c), t2.reshape(b, n1h, n2 * c), scale)
    return y.reshape(b, l, c)


def even_mixer(h, z_w, p, i, q_w):
    b, l, d = h.shape
    c = d - q_w
    in_w = z_w.shape[1]
    kv_w = (in_w - q_w - 3 * c) // 2
    z = matmul(h.reshape(b * l, d), z_w, name="even_in_proj").reshape(b, l, in_w)
    att = banded_attention(z, p["ev_sinks"][i], q_w, kv_w)
    w16, x0, t2 = hyena_prologue(z, p["ev_conv_w"][i], p["ev_conv_b"][i][None], p["ev_hy_bias"][i][None],
                                 q_w + 2 * kv_w, c)
    hfilt, norm = hyena_filter(l, p["ev_filt_w1"][i], p["ev_filt_b1"][i][None], p["ev_filt_f1"][i][None],
                               p["ev_filt_w2"][i], p["ev_filt_b2"][i][None], p["ev_filt_f2"][i][None],
                               p["ev_filt_w3"][i], c)
    y_hy = hyena_long_conv(w16, x0, t2, hfilt, norm)
    return jnp.concatenate([att, y_hy], axis=-1).reshape(b * l, d)


def _rw_mix_kernel(h_ref, p_ref, n_ref, lerp_ref, *o_refs):
    h = h_ref[...]
    prev_row, next_row = _halo_rows(p_ref, n_ref)
    up, dn = _shift_rows(h, prev_row, next_row)
    xx = 0.5 * (up + dn) - h
    for n, o_ref in enumerate(o_refs):
        o_ref[...] = (h + xx * lerp_ref[n:n + 1, :]).astype(o_ref.dtype)


def rwkv_mix(h, lerp):
    b, l, d = h.shape
    n = lerp.shape[0]
    tl = _tile(l, 256, SUBLANES)
    tc = _tile(d, 512, LANES)
    nh = l // SUBLANES
    tps = tl // SUBLANES
    blk = pl.BlockSpec((None, tl, tc), lambda bi, i, j: (bi, i, j))
    return pl.pallas_call(
        _rw_mix_kernel,
        out_shape=[jax.ShapeDtypeStruct((b, l, d), BF16)] * n,
        grid=(b, l // tl, d // tc),
        in_specs=[blk,
                  pl.BlockSpec((None, SUBLANES, tc), lambda bi, i, j: (bi, jnp.maximum(i * tps - 1, 0), j)),
                  pl.BlockSpec((None, SUBLANES, tc), lambda bi, i, j: (bi, jnp.minimum((i + 1) * tps, nh - 1), j)),
                  pl.BlockSpec((n, tc), lambda bi, i, j: (0, j))],
        out_specs=[blk] * n,
        compiler_params=_cparams("parallel", "parallel", "parallel"),
        name="rwkv_mix",
    )(h, h, h, lerp)


def _head_sum(x, m0):
    s0 = jnp.sum(jnp.where(m0, x, 0.0), axis=-1, keepdims=True)
    s1 = jnp.sum(jnp.where(m0, 0.0, x), axis=-1, keepdims=True)
    return jnp.where(m0, s0, s1)


def _wkv_kernel(r_ref, k_ref, v_ref, lw_ref, a_ref, kk_ref, ka_ref, rk_ref, y_ref, bon_ref, s_ref, *, rev):
    @pl.when(pl.program_id(2) == 0)
    def _():
        s_ref[...] = jnp.zeros_like(s_ref)

    pair = 2 * RW_HEAD
    pairs = range(s_ref.shape[0])
    c = r_ref.shape[0]
    c2 = 2 * c
    prep = [_wkv_prep(*(ref[:, g * pair:(g + 1) * pair] for ref in
                        (r_ref, k_ref, v_ref, lw_ref, a_ref, kk_ref, ka_ref, rk_ref)), rev) for g in pairs]
    for g in pairs:
        bon_ref[:, g * pair:(g + 1) * pair] = prep[g]["bonus"]
    state = [s_ref[g] for g in pairs]

    row2 = lax.broadcasted_iota(jnp.int32, (c2, c2), 0)
    col2 = lax.broadcasted_iota(jnp.int32, (c2, c2), 1)
    strict = (col2 > row2) if rev else (col2 < row2)
    incl = (col2 >= row2) if rev else (col2 <= row2)
    m_all = [_bdot_nt(jnp.concatenate([p["aq"], p["rq"]], axis=0), jnp.concatenate([p["bd"], p["kd"]], axis=0))
             for p in prep]
    m_ab = [jnp.where(strict, m[:c2, :c2], 0.0) for m in m_all]
    m_ak = [jnp.where(strict, m[:c2, c2:], 0.0) for m in m_all]
    m_rb = [jnp.where(incl, m[c2:, :c2], 0.0) for m in m_all]
    m_rk = [jnp.where(incl, m[c2:, c2:], 0.0) for m in m_all]

    eye = jnp.where(row2 == col2, 1.0, 0.0)
    x = [eye + m for m in m_ab]
    pw = m_ab
    for _ in range(int(math.log2(c)) - 1):
        pw = [_bdot(p, p) for p in pw]
        x = [xi + _bdot(xi, p) for xi, p in zip(x, pw)]

    rhs = [_bdot_nt(p["aq"], s) + _bdot(m, p["v"]) for p, s, m in zip(prep, state, m_ak)]
    ys = [_bdot_nt(p["rq"], s) + _bdot(m, p["v"]) for p, s, m in zip(prep, state, m_rk)]
    us = [_bdot(xi, r) for xi, r in zip(x, rhs)]
    ys = [y + _bdot(m, u) for y, m, u in zip(ys, m_rb, us)]
    upd = [lax.dot_general(jnp.concatenate([p["v"], u], axis=0).astype(BF16),
                           jnp.concatenate([p["kc"], p["bc"]], axis=0).astype(BF16),
                           (((0,), (0,)), ((), ())), preferred_element_type=F32) for p, u in zip(prep, us)]
    for g in pairs:
        y_ref[:, g * pair:(g + 1) * pair] = ys[g][:c] + ys[g][c:]
        s_ref[g] = state[g] * prep[g]["decay"] + upd[g]


def _wkv_prep(r, k, v, lw, a, k_k, k_a, r_k, rev):
    c = r.shape[0]
    m0 = lax.broadcasted_iota(jnp.int32, r.shape, 1) < RW_HEAD
    kk = k * k_k
    kk = kk / jnp.maximum(jnp.sqrt(_head_sum(kk * kk, m0)), 1e-12)
    kd = k * (1.0 + (a - 1.0) * k_a)
    bvec = kk * a

    row = lax.broadcasted_iota(jnp.int32, (c, c), 0)
    col = lax.broadcasted_iota(jnp.int32, (c, c), 1)
    tri = (col >= row) if rev else (col <= row)
    ci = _hdot(tri.astype(F32), lw)
    ctot = jnp.sum(lw, axis=0, keepdims=True)
    inv = jnp.exp(-ci)
    tail = jnp.exp(ctot - ci)

    def stack(x):
        return jnp.concatenate([jnp.where(m0, x, 0.0), jnp.where(m0, 0.0, x)], axis=0)

    return dict(aq=stack(-kk * jnp.exp(ci - lw)), rq=stack(r * jnp.exp(ci)), kd=stack(kd * inv), bd=stack(bvec * inv),
                kc=stack(kd * tail), bc=stack(bvec * tail), v=stack(v), decay=jnp.exp(ctot),
                bonus=_head_sum(r * kd * r_k, m0) * v)


def wkv_scan(r, k, v, lw, a, k_k, k_a, r_k, rev):
    b, l, d = r.shape
    c = RW_CHUNK
    nch = l // c
    pair = 2 * RW_HEAD
    npairs = d // pair
    group = next(g for g in (WKV_PAIRS_PER_STEP, 3, 2, 1) if npairs % g == 0)
    wide = group * pair
    cidx = (lambda ci: nch - 1 - ci) if rev else (lambda ci: ci)
    seq = pl.BlockSpec((None, c, wide), lambda bi, hp, ci: (bi, cidx(ci), hp))
    par = pl.BlockSpec((1, wide), lambda bi, hp, ci: (0, hp))
    out = jax.ShapeDtypeStruct((b, l, d), F32)
    return pl.pallas_call(
        functools.partial(_wkv_kernel, rev=rev),
        out_shape=[out, out],
        grid=(b, npairs // group, nch),
        in_specs=[seq] * 5 + [par] * 3,
        out_specs=[seq, seq],
        scratch_shapes=[pltpu.VMEM((group, pair, pair), F32)],
        compiler_params=_cparams("parallel", "parallel", "arbitrary"),
        name="wkv_scan_bwd" if rev else "wkv_scan_fwd",
    )(r, k, v, lw, a, k_k, k_a, r_k)


def _rw_post_kernel(yf_ref, yb_ref, bf_ref, bb_ref, g_ref, gg_ref, gb_ref, o_ref):
    y = yf_ref[...] + yb_ref[...]
    m0 = lax.broadcasted_iota(jnp.int32, y.shape, 1) < RW_HEAD
    mu = _head_sum(y, m0) * (1.0 / RW_HEAD)
    dlt = y - mu
    var = _head_sum(dlt * dlt, m0) * (1.0 / RW_HEAD)
    yn = dlt * lax.rsqrt(var + RW_GN_EPS) * gg_ref[...] + gb_ref[...] + bf_ref[...] + bb_ref[...]
    o_ref[...] = (yn * g_ref[...]).astype(o_ref.dtype)


def rwkv_post(yf, yb, bf, bb, g, gn_g, gn_b):
    b, l, d = yf.shape
    pair = 2 * RW_HEAD
    tl = _tile(l, 1024, SUBLANES)
    blk = pl.BlockSpec((None, tl, pair), lambda bi, i, j: (bi, i, j))
    par = pl.BlockSpec((1, pair), lambda bi, i, j: (0, j))
    return pl.pallas_call(
        _rw_post_kernel,
        out_shape=jax.ShapeDtypeStruct((b, l, d), BF16),
        grid=(b, l // tl, d // pair),
        in_specs=[blk] * 5 + [par, par],
        out_specs=blk,
        compiler_params=_cparams("parallel", "parallel", "parallel"),
        name="rwkv_post",
    )(yf, yb, bf, bb, g, gn_g, gn_b)


def rwkv_mixer(h, p, j):
    b, l, d = h.shape
    t = b * l
    xr, xw, xk, xv, xa, xg = (x.reshape(t, d) for x in rwkv_mix(h, p["od_lerp"][j]))
    r = matmul(xr, p["od_w_r"][j], name="rwkv_r")
    k = matmul(xk, p["od_w_k"][j], name="rwkv_k")
    v = matmul(xv, p["od_w_v"][j], name="rwkv_v")
    g = matmul(matmul(xg, p["od_g1"][j], name="rwkv_g1"), p["od_g2"][j], in_act="sigmoid", name="rwkv_g2")
    w1 = jnp.concatenate([p["od_w1"][j][0], p["od_w1"][j][1]], axis=1)
    a1 = jnp.concatenate([p["od_a1"][j][0], p["od_a1"][j][1]], axis=1)
    tw = matmul(xw, w1, name="rwkv_w1")
    ta = matmul(xa, a1, name="rwkv_a1")
    nl = tw.shape[1] // 2
    na = ta.shape[1] // 2
    shp = (b, l, d)
    ys, bons = [], []
    for di, rev in ((0, False), (1, True)):
        lw = matmul(tw[:, di * nl:(di + 1) * nl], p["od_w2"][j][di], bias=p["od_w0"][j][di][None], in_act="tanh",
                    out_act="log_decay", name="rwkv_w2")
        a = matmul(ta[:, di * na:(di + 1) * na], p["od_a2"][j][di], bias=p["od_a0"][j][di][None],
                   out_act="sigmoid", name="rwkv_a2")
        y, bon = wkv_scan(r.reshape(shp), k.reshape(shp), v.reshape(shp), lw.reshape(shp), a.reshape(shp),
                          p["od_k_k"][j][None], p["od_k_a"][j][None], p["od_r_k"][j].reshape(1, d), rev)
        ys.append(y)
        bons.append(bon)
    out = rwkv_post(ys[0], ys[1], bons[0], bons[1], g.reshape(shp), p["od_gn_g"][j][None], p["od_gn_b"][j][None])
    return out.reshape(t, d)


def _top_values(s, k):
    rows = lax.broadcasted_iota(jnp.int32, (k, s.shape[1]), 0)

    def body(i, carry):
        s, vals = carry
        m = jnp.max(s, axis=0, keepdims=True)
        return jnp.where(s >= m, -jnp.inf, s), jnp.where(rows == i, m, vals)

    return lax.fori_loop(0, k, body, (s, jnp.full((k, s.shape[1]), -jnp.inf, F32)))[1]


def _peer_topk_kernel(q_ref, sub_ref, s_ref, e_ref, tau_ref):
    k = PEER_TOPK
    for h in range(tau_ref.shape[0]):
        s1 = _bdot(sub_ref[2 * h], q_ref[2 * h])
        s2 = _bdot(sub_ref[2 * h + 1], q_ref[2 * h + 1])
        v1 = _top_values(s1, k)
        v2 = _top_values(s2, k)
        cand = jnp.concatenate([v1[i:i + 1, :] + v2 for i in range(k)], axis=0)
        top = _top_values(cand, k)
        z = jnp.sum(jnp.exp(top - top[0:1, :]), axis=0, keepdims=True)
        s_ref[2 * h] = s1
        s_ref[2 * h + 1] = s2
        e_ref[2 * h] = jnp.exp(s1 - v1[0:1, :])
        e_ref[2 * h + 1] = jnp.exp(s2 - v2[0:1, :]) / z
        tau_ref[h:h + 1, :] = top[k - 1:k, :]


def peer_topk(q_t, sub):
    hp, dk, t = q_t.shape
    nk = sub.shape[1]
    tm = _tile(t, 512, LANES)
    blk = pl.BlockSpec((hp, nk, tm), lambda i: (0, 0, i))
    big = jax.ShapeDtypeStruct((hp, nk, t), F32)
    return pl.pallas_call(
        _peer_topk_kernel,
        out_shape=[big, big, jax.ShapeDtypeStruct((hp // 2, t), F32)],
        grid=(t // tm,),
        in_specs=[pl.BlockSpec((hp, dk, tm), lambda i: (0, 0, i)), pl.BlockSpec((hp, nk, dk), lambda i: (0, 0, 0))],
        out_specs=[blk, blk, pl.BlockSpec((hp // 2, tm), lambda i: (0, i))],
        compiler_params=_cparams("parallel"),
        name="peer_topk",
    )(q_t, sub)


def _peer_gate(s_ref, e_ref, tau_ref, a):
    gate = jnp.zeros(s_ref.shape[1:], F32)
    for h in range(tau_ref.shape[0]):
        s1 = s_ref[2 * h, pl.ds(a, 1), :]
        e1 = e_ref[2 * h, pl.ds(a, 1), :]
        hit = (s1 + s_ref[2 * h + 1]) >= tau_ref[h:h + 1, :]
        gate = gate + jnp.where(hit, e1 * e_ref[2 * h + 1], 0.0)
    return gate


def _gelu(x):
    return 0.5 * x * (1.0 + lax.erf(x * (2.0 ** -0.5)))


def _peer_main_kernel(u_ref, x_ref, v_ref, s_ref, e_ref, tau_ref, o_ref, acc_ref, *, na):
    j = pl.program_id(1)
    nk = s_ref.shape[1]
    ts = na * nk

    @pl.when(j == 0)
    def _():
        acc_ref[...] = jnp.zeros_like(acc_ref)

    acts = [_bdot(u_ref[i * ts:(i + 1) * ts, :], x_ref[...]) for i in range(2)]
    for i in range(2):
        act = _gelu(acts[i])
        pieces = [(_peer_gate(s_ref, e_ref, tau_ref, (2 * j + i) * na + al) * act[al * nk:(al + 1) * nk, :]).astype(BF16)
                  for al in range(na)]
        acc_ref[...] += jnp.dot(v_ref[:, i * ts:(i + 1) * ts], jnp.concatenate(pieces, axis=0),
                                preferred_element_type=F32)

    @pl.when(j == pl.num_programs(1) - 1)
    def _():
        d = o_ref.shape[1]
        step = _tile(d, o_ref.shape[0], LANES)
        for c0 in range(0, d, step):
            o_ref[:, c0:c0 + step] = acc_ref[c0:c0 + step, :].T


def peer_main(u, x_t, v_t, s, e, tau):
    ne, d = u.shape
    t = x_t.shape[1]
    nk = s.shape[1]
    tm = _tile(t, 512, LANES)
    na = PEER_KEYS_PER_SUBTILE
    te = 2 * na * nk
    once = pl.Buffered(1)
    tok = lambda shape: pl.BlockSpec(shape, lambda i, j: (0,) * (len(shape) - 1) + (i,), pipeline_mode=once)
    return pl.pallas_call(
        functools.partial(_peer_main_kernel, na=na),
        out_shape=jax.ShapeDtypeStruct((t, d), F32),
        grid=(t // tm, ne // te),
        in_specs=[pl.BlockSpec((te, d), lambda i, j: (j, 0)), tok((d, tm)), pl.BlockSpec((d, te), lambda i, j: (0, j)),
                  tok(s.shape[:2] + (tm,)), tok(e.shape[:2] + (tm,)), tok((tau.shape[0], tm))],
        out_specs=pl.BlockSpec((tm, d), lambda i, j: (i, 0), pipeline_mode=once),
        scratch_shapes=[pltpu.VMEM((d, tm), F32)],
        compiler_params=_cparams("parallel", "arbitrary"),
        name="peer_main",
    )(u, x_t, v_t, s, e, tau)


def peer(x_t, wq_t, sub, u, v_t):
    q_t = matmul(wq_t, x_t, name="peer_query")
    hp, nk, dk = sub.shape
    s, e, tau = peer_topk(q_t.reshape(hp, dk, -1), sub)
    return peer_main(u, x_t, v_t, s, e, tau)


def _prepare(w):
    p = dict(w)
    for name in ("ev_w_in", "ev_w_out", "od_w_r", "od_w_k", "od_w_v", "od_w_o", "od_w1", "od_w2", "od_a1", "od_a2",
                 "od_g1", "od_g2", "pk_u"):
        p[name] = w[name].astype(BF16)
    p["pk_v_t"] = jnp.swapaxes(w["pk_v"].astype(BF16), 1, 2)
    p["pk_wq_t"] = jnp.swapaxes(w["pk_w_q"].astype(BF16), 1, 2)
    sk = w["pk_sub_keys"]
    p["pk_sub"] = sk.astype(BF16).reshape(sk.shape[0], sk.shape[1] * sk.shape[2], sk.shape[3], sk.shape[4])
    return p


def _trunk(x, mods, p):
    b, l, d = x.shape
    t = b * l
    depth = p["ada_w"].shape[0]
    q_w = p["ev_sinks"].shape[1] * ATT_HEAD_DIM
    y = gt2 = None
    for layer in range(depth):
        sh1, sc1, gt1, sh2, sc2, gt2_next = mods[layer]
        even = layer % 2 == 0
        g1 = p["norm_g"][layer, 0][None]
        kw = dict(shift=sh1, scale=sc1, out_dtype=BF16 if even else F32)
        if y is None:
            h = norm_mod(x, g1, **kw)
        else:
            x, h = norm_mod(x, g1, res=y, gate=gt2, emit_x=True, **kw)
        if even:
            pre, w_out = even_mixer(h, p["ev_w_in"][layer // 2], p, layer // 2, q_w), p["ev_w_out"][layer // 2]
        else:
            pre, w_out = rwkv_mixer(h, p, layer // 2), p["od_w_o"][layer // 2]
        x = matmul(pre, w_out, res=x.reshape(t, d), gate=gt1, rows_per_gate=l, name="mixer_out_proj").reshape(b, l, d)
        h2_t = norm_mod(x, p["norm_g"][layer, 1][None], shift=sh2, scale=sc2, transposed=True)
        y = peer(h2_t, p["pk_wq_t"][layer], p["pk_sub"][layer], p["pk_u"][layer], p["pk_v_t"][layer])
        y = y.reshape(b, l, d)
        gt2 = gt2_next
    return norm_mod(x, p["final_g"][None], res=y, gate=gt2, out_dtype=F32)


def kernel(x_prompt, x_sample, c_prompt, c_sample, ada_w, ada_b, norm_g, final_g, ev_w_in, ev_sinks, ev_conv_w, ev_conv_b, ev_filt_w1, ev_filt_b1, ev_filt_f1, ev_filt_w2, ev_filt_b2, ev_filt_f2, ev_filt_w3, ev_hy_bias, ev_w_out, od_lerp, od_w_r, od_w_k, od_w_v, od_w_o, od_w0, od_w1, od_w2, od_a0, od_a1, od_a2, od_g1, od_g2, od_k_k, od_k_a, od_r_k, od_gn_g, od_gn_b, pk_w_q, pk_sub_keys, pk_u, pk_v):
    p = _prepare(dict(
        ada_w=ada_w, ada_b=ada_b, norm_g=norm_g, final_g=final_g, ev_w_in=ev_w_in, ev_sinks=ev_sinks,
        ev_conv_w=ev_conv_w, ev_conv_b=ev_conv_b, ev_filt_w1=ev_filt_w1, ev_filt_b1=ev_filt_b1, ev_filt_f1=ev_filt_f1,
        ev_filt_w2=ev_filt_w2, ev_filt_b2=ev_filt_b2, ev_filt_f2=ev_filt_f2, ev_filt_w3=ev_filt_w3,
        ev_hy_bias=ev_hy_bias, ev_w_out=ev_w_out, od_lerp=od_lerp, od_w_r=od_w_r, od_w_k=od_w_k, od_w_v=od_w_v,
        od_w_o=od_w_o, od_w0=od_w0, od_w1=od_w1, od_w2=od_w2, od_a0=od_a0, od_a1=od_a1, od_a2=od_a2, od_g1=od_g1,
        od_g2=od_g2, od_k_k=od_k_k, od_k_a=od_k_a, od_r_k=od_r_k, od_gn_g=od_gn_g, od_gn_b=od_gn_b,
        pk_w_q=pk_w_q, pk_sub_keys=pk_sub_keys, pk_u=pk_u, pk_v=pk_v))
    depth, d = ada_w.shape[0], ada_w.shape[1]
    bp, bs = c_prompt.shape[0], c_sample.shape[0]
    rows = -(-(bp + bs) // SUBLANES) * SUBLANES
    c_all = jnp.pad(jnp.concatenate([c_prompt, c_sample], axis=0), ((0, rows - bp - bs), (0, 0)))
    mods_p, mods_s = [], []
    for layer in range(depth):
        mod = matmul(c_all, ada_w[layer], bias=ada_b[layer][None], in_act="silu", name="adaln")
        parts = jnp.split(mod, 6, axis=-1)
        mods_p.append([m[:bp, None, :] for m in parts])
        mods_s.append([m[bp:bp + bs, None, :] for m in parts])
    return _trunk(x_prompt, mods_p, p), _trunk(x_sample, mods_s, p)
```

```python
import functools
import math

import jax
import jax.numpy as jnp
import numpy as np
from jax import lax
from jax.experimental import pallas as pl
from jax.experimental.pallas import tpu as pltpu

F32 = jnp.float32
BF16 = jnp.bfloat16

LANES = 128
SUBLANES = 8
VMEM_LIMIT_BYTES = 56 * 1024 * 1024

NORM_EPS = 1e-6
ATT_HEAD_DIM = 128
ATT_GROUP = 4
WINDOW = 128
ROPE_THETA = 10000.0
HY_BANDS = 16
HY_DECAY_TARGET = 1e-2
HY_FAST_DECAY = 0.3
HY_SLOW_DECAY = 1.5
HY_MOD_SHIFT = 0.05
RW_HEAD = 64
RW_GN_EPS = 64e-5
RW_CHUNK = 64
WKV_PAIRS_PER_STEP = 8
PEER_NKEYS = 128
PEER_TOPK = 16
PEER_KEYS_PER_SUBTILE = 2
DFT_N2 = 128


def _cparams(*sem):
    return pltpu.CompilerParams(dimension_semantics=sem, vmem_limit_bytes=VMEM_LIMIT_BYTES)


def _tile(n, target, mult):
    t = min(n, target)
    t -= t % mult
    while t >= mult:
        if n % t == 0:
            return t
        t -= mult
    return n


def _bdot(a, b):
    return jnp.dot(a.astype(BF16), b.astype(BF16), preferred_element_type=F32)


def _bdot_nt(a, b):
    return lax.dot_general(a.astype(BF16), b.astype(BF16), (((1,), (1,)), ((), ())), preferred_element_type=F32)


def _hdot(a, b):
    return jnp.dot(a.astype(F32), b.astype(F32), preferred_element_type=F32, precision=lax.Precision.HIGHEST)


def _silu(x):
    return x * jax.nn.sigmoid(x)


def _log_decay(z):
    return -jnp.exp(-jax.nn.softplus(-z) - 0.5)


_ACTS = {None: None, "silu": _silu, "tanh": jnp.tanh, "sigmoid": jax.nn.sigmoid, "log_decay": _log_decay}


def _mm_kernel(*refs, in_act, out_act, has_bias, has_res):
    x_ref, w_ref = refs[0], refs[1]
    pos = 2
    bias_ref = res_ref = gate_ref = None
    if has_bias:
        bias_ref = refs[pos]
        pos += 1
    if has_res:
        res_ref, gate_ref = refs[pos], refs[pos + 1]
        pos += 2
    o_ref = refs[pos]
    x = x_ref[...]
    if in_act is not None:
        x = _ACTS[in_act](x.astype(F32))
    acc = _bdot(x, w_ref[...])
    if has_bias:
        acc = acc + bias_ref[...]
    if out_act is not None:
        acc = _ACTS[out_act](acc)
    if has_res:
        acc = res_ref[...] + gate_ref[...] * acc
    o_ref[...] = acc.astype(o_ref.dtype)


def matmul(x, w, *, bias=None, in_act=None, out_act=None, res=None, gate=None, rows_per_gate=None,
           out_dtype=F32, tm=1024, tn=512, layer=None, name="matmul"):
    m, k = x.shape[-2:]
    k2, n = w.shape[-2:]
    assert k == k2
    has_res = res is not None
    tm = _tile(rows_per_gate if has_res else m, tm, SUBLANES)
    tn = _tile(n, tn, LANES)
    assert m % tm == 0
    x_spec = (pl.BlockSpec((tm, k), lambda j, i: (i, 0)) if x.ndim == 2
              else pl.BlockSpec((None, tm, k), lambda j, i: (layer, i, 0)))
    w_spec = (pl.BlockSpec((k, tn), lambda j, i: (0, j)) if w.ndim == 2
              else pl.BlockSpec((None, k, tn), lambda j, i: (layer, 0, j)))
    in_specs = [x_spec, w_spec]
    args = [x, w]
    if bias is not None:
        in_specs.append(pl.BlockSpec((1, tn), lambda j, i: (0, j)))
        args.append(bias)
    if has_res:
        rpg = rows_per_gate // tm
        in_specs.append(pl.BlockSpec((tm, tn), lambda j, i: (i, j)))
        in_specs.append(pl.BlockSpec((None, 1, tn), lambda j, i: (i // rpg, 0, j)))
        args += [res, gate]
    return pl.pallas_call(
        functools.partial(_mm_kernel, in_act=in_act, out_act=out_act, has_bias=bias is not None, has_res=has_res),
        out_shape=jax.ShapeDtypeStruct((m, n), out_dtype),
        grid=(n // tn, m // tm),
        in_specs=in_specs,
        out_specs=pl.BlockSpec((tm, tn), lambda j, i: (i, j)),
        compiler_params=_cparams("parallel", "parallel"),
        name=name,
    )(*args)


def _norm_kernel(*refs, has_res, has_mod, emit_x, transposed):
    pos = 0
    x = refs[pos][...]
    pos += 1
    if has_res:
        x = x + refs[pos + 1][...] * refs[pos][...]
        pos += 2
    g = refs[pos][...]
    pos += 1
    y = x * lax.rsqrt(jnp.mean(x * x, axis=-1, keepdims=True) + NORM_EPS) * g
    if has_mod:
        y = y * (1.0 + refs[pos + 1][...]) + refs[pos][...]
        pos += 2
    if emit_x:
        refs[pos][...] = x
        pos += 1
    if transposed:
        y = y.T
    refs[pos][...] = y.astype(refs[pos].dtype)


def norm_mod(x, g, *, shift=None, scale=None, res=None, gate=None, emit_x=False, out_dtype=BF16, tl=256,
             transposed=False):
    b, l, d = x.shape
    tl = _tile(l, tl, LANES if transposed else SUBLANES)
    nl = l // tl
    row = pl.BlockSpec((None, tl, d), lambda bi, i: (bi, i, 0))
    vec = pl.BlockSpec((None, 1, d), lambda bi, i: (bi, 0, 0))
    in_specs, args = [row], [x]
    if res is not None:
        in_specs += [row, vec]
        args += [res, gate]
    in_specs.append(pl.BlockSpec((1, d), lambda bi, i: (0, 0)))
    args.append(g)
    if shift is not None:
        in_specs += [vec, vec]
        args += [shift, scale]
    if transposed:
        out_shape = [jax.ShapeDtypeStruct((d, b * l), out_dtype)]
        out_specs = [pl.BlockSpec((d, tl), lambda bi, i: (0, bi * nl + i))]
    else:
        out_shape = [jax.ShapeDtypeStruct((b, l, d), out_dtype)]
        out_specs = [row]
    if emit_x:
        out_shape.insert(0, jax.ShapeDtypeStruct((b, l, d), F32))
        out_specs.insert(0, row)
    outs = pl.pallas_call(
        functools.partial(_norm_kernel, has_res=res is not None, has_mod=shift is not None, emit_x=emit_x,
                          transposed=transposed),
        out_shape=out_shape,
        grid=(b, nl),
        in_specs=in_specs,
        out_specs=out_specs,
        compiler_params=_cparams("parallel", "parallel"),
        name="norm_mod",
    )(*args)
    return outs if emit_x else outs[0]


def _rope(x, cos2, sin2):
    return x * cos2 + pltpu.roll(x, ATT_HEAD_DIM // 2, axis=1) * sin2


def _attn_kernel(sink_ref, q_ref, kp_ref, kc_ref, kn_ref, vp_ref, vc_ref, vn_ref,
                 cq_ref, sq_ref, cp_ref, sp_ref, cn_ref, sn_ref, o_ref, *, seq_len):
    n = pl.program_id(1)
    h = pl.program_id(2)
    blk = WINDOW
    k3 = jnp.concatenate([_rope(kp_ref[...], cp_ref[...], sp_ref[...]),
                          _rope(kc_ref[...], cq_ref[...], sq_ref[...]),
                          _rope(kn_ref[...], cn_ref[...], sn_ref[...])], axis=0).astype(BF16)
    v3 = jnp.concatenate([vp_ref[...], vc_ref[...], vn_ref[...]], axis=0).astype(BF16)
    qpos = n * blk + lax.broadcasted_iota(jnp.int32, (blk, 3 * blk), 0)
    kpos = (n - 1) * blk + lax.broadcasted_iota(jnp.int32, (blk, 3 * blk), 1)
    valid = (jnp.abs(qpos - kpos) <= WINDOW) & (kpos >= 0) & (kpos < seq_len)
    heads = range(ATT_GROUP)
    qs = [_rope(q_ref[:, g * ATT_HEAD_DIM:(g + 1) * ATT_HEAD_DIM], cq_ref[...], sq_ref[...]) for g in heads]
    ss = [jnp.where(valid, _bdot_nt(q, k3) * (ATT_HEAD_DIM ** -0.5), -jnp.inf) for q in qs]
    ps = []
    for g in heads:
        sink = sink_ref[h * ATT_GROUP + g]
        m = jnp.maximum(jnp.max(ss[g], axis=-1, keepdims=True), sink)
        p = jnp.exp(ss[g] - m)
        ps.append(p / (jnp.sum(p, axis=-1, keepdims=True) + jnp.exp(sink - m)))
    for g in heads:
        o_ref[:, g * ATT_HEAD_DIM:(g + 1) * ATT_HEAD_DIM] = _bdot(ps[g], v3).astype(o_ref.dtype)


def banded_attention(z, sinks, q_w, kv_w):
    b, l, _ = z.shape
    blk = WINDOW
    nb = l // blk
    hd = ATT_HEAD_DIM
    kvh = kv_w // hd
    gw = ATT_GROUP * hd
    half = hd // 2
    inv = ROPE_THETA ** (-jnp.arange(half, dtype=F32) * 2.0 / hd)
    ang = jnp.arange(l, dtype=F32)[:, None] * inv[None, :]
    cos2 = jnp.concatenate([jnp.cos(ang), jnp.cos(ang)], axis=-1)
    sin2 = jnp.concatenate([-jnp.sin(ang), jnp.sin(ang)], axis=-1)
    kcol, vcol = q_w // hd, (q_w + kv_w) // hd

    def prev(i):
        return jnp.maximum(i - 1, 0)

    def nxt(i):
        return jnp.minimum(i + 1, nb - 1)

    def kv_spec(col0, which):
        return pl.BlockSpec((None, blk, hd), lambda bi, i, h: (bi, which(i), col0 + h))

    def tab_spec(which):
        return pl.BlockSpec((blk, hd), lambda bi, i, h: (which(i), 0))

    same = lambda i: i
    return pl.pallas_call(
        functools.partial(_attn_kernel, seq_len=l),
        out_shape=jax.ShapeDtypeStruct((b, l, q_w), BF16),
        grid=(b, nb, kvh),
        in_specs=[pl.BlockSpec(memory_space=pltpu.SMEM),
                  pl.BlockSpec((None, blk, gw), lambda bi, i, h: (bi, i, h)),
                  kv_spec(kcol, prev), kv_spec(kcol, same), kv_spec(kcol, nxt),
                  kv_spec(vcol, prev), kv_spec(vcol, same), kv_spec(vcol, nxt),
                  tab_spec(same), tab_spec(same), tab_spec(prev), tab_spec(prev), tab_spec(nxt), tab_spec(nxt)],
        out_specs=pl.BlockSpec((None, blk, gw), lambda bi, i, h: (bi, i, h)),
        compiler_params=_cparams("parallel", "parallel", "parallel"),
        name="banded_attention",
    )(sinks, z, z, z, z, z, z, z, cos2, sin2, cos2, sin2, cos2, sin2)


def _shift_rows(x, prev_row, next_row):
    tl = x.shape[0]
    row = lax.broadcasted_iota(jnp.int32, x.shape, 0)
    up = jnp.where(row == 0, prev_row, pltpu.roll(x, 1, axis=0))
    dn = jnp.where(row == tl - 1, next_row, pltpu.roll(x, tl - 1, axis=0))
    return up, dn


def _halo_rows(prev_ref, next_ref):
    i = pl.program_id(1)
    last = pl.num_programs(1) - 1
    prev_row = jnp.where(i == 0, 0.0, prev_ref[SUBLANES - 1:SUBLANES, :])
    next_row = jnp.where(i == last, 0.0, next_ref[0:1, :])
    return prev_row, next_row


def _hy_pro_kernel(*refs):
    groups = [refs[3 * g:3 * g + 3] for g in range(3)]
    cw_refs = refs[9:12]
    cb_refs = refs[12:15]
    bias_ref = refs[15]
    w_ref, x0_ref, t2_ref = refs[16:19]
    u = []
    for (c_ref, p_ref, n_ref), cw_ref, cb_ref in zip(groups, cw_refs, cb_refs):
        x = c_ref[...]
        prev_row, next_row = _halo_rows(p_ref, n_ref)
        up, dn = _shift_rows(x, prev_row, next_row)
        u.append(up * cw_ref[0:1, :] + x * cw_ref[1:2, :] + dn * cw_ref[2:3, :] + cb_ref[...])
    x0, x1, hv = u
    w = hv * x1
    w_ref[...] = w.astype(w_ref.dtype)
    x0_ref[...] = x0.astype(x0_ref.dtype)
    t2_ref[...] = (x0 * (w * bias_ref[...])).astype(t2_ref.dtype)


def hyena_prologue(z, conv_w, conv_b, hy_bias, col0, c):
    b, l, _ = z.shape
    tl = _tile(l, 256, SUBLANES)
    tc = _tile(c, 512, LANES)
    nh = l // SUBLANES
    tps = tl // SUBLANES
    in_specs, args = [], []
    for g in range(3):
        cb0 = (col0 + g * c) // tc
        in_specs += [
            pl.BlockSpec((None, tl, tc), lambda bi, i, j, cb0=cb0: (bi, i, cb0 + j)),
            pl.BlockSpec((None, SUBLANES, tc), lambda bi, i, j, cb0=cb0: (bi, jnp.maximum(i * tps - 1, 0), cb0 + j)),
            pl.BlockSpec((None, SUBLANES, tc), lambda bi, i, j, cb0=cb0: (bi, jnp.minimum((i + 1) * tps, nh - 1), cb0 + j)),
        ]
        args += [z, z, z]
    for g in range(3):
        in_specs.append(pl.BlockSpec((3, tc), lambda bi, i, j, g=g: (0, g * (c // tc) + j)))
        args.append(conv_w)
    for g in range(3):
        in_specs.append(pl.BlockSpec((1, tc), lambda bi, i, j, g=g: (0, g * (c // tc) + j)))
        args.append(conv_b)
    in_specs.append(pl.BlockSpec((1, tc), lambda bi, i, j: (0, j)))
    args.append(hy_bias)
    out_spec = pl.BlockSpec((None, tl, tc), lambda bi, i, j: (bi, i, j))
    return pl.pallas_call(
        _hy_pro_kernel,
        out_shape=[jax.ShapeDtypeStruct((b, l, c), BF16)] * 3,
        grid=(b, l // tl, c // tc),
        in_specs=in_specs,
        out_specs=[out_spec] * 3,
        compiler_params=_cparams("parallel", "parallel", "parallel"),
        name="hyena_prologue",
    )(*args)


def _hy_filter_kernel(z_ref, w1_ref, b1_ref, f1_ref, w2_ref, b2_ref, f2_ref, w3_ref, dl_ref,
                      h_ref, asum_ref, row0_ref, *, seq_len):
    i = pl.program_id(0)
    tl = z_ref.shape[0]
    c = dl_ref.shape[1]
    h1 = jnp.sin(f1_ref[...] * (_hdot(z_ref[...], w1_ref[...]) + b1_ref[...]))
    h2 = jnp.sin(f2_ref[...] * (_hdot(h1, w2_ref[...]) + b2_ref[...]))
    h3 = _hdot(h2, w3_ref[...])
    row = i * tl + lax.broadcasted_iota(jnp.int32, (tl, 1), 0)
    t = row.astype(F32) * (1.0 / (seq_len - 1))
    window = jnp.exp(-t * dl_ref[...]) + HY_MOD_SHIFT
    h3 = h3 * jnp.concatenate([window, window], axis=1)
    h_ref[...] = h3

    @pl.when(i == 0)
    def _():
        asum_ref[...] = jnp.zeros_like(asum_ref)
        row0_ref[...] = h3[0:1, :]

    asum_ref[...] += jnp.sum(jnp.where(row == 0, 0.0, jnp.abs(h3)), axis=0, keepdims=True)


def hyena_filter(l, w1, b1, f1, w2, b2, f2, w3, c):
    t = jnp.linspace(0.0, 1.0, l, dtype=F32)[:, None]
    w = 2.0 * math.pi * jnp.arange(l, dtype=F32) / l
    bands = jnp.linspace(1e-4, HY_BANDS - 1, HY_BANDS, dtype=F32)
    ang = w[:, None] * bands[None, :]
    z = jnp.concatenate([t, jnp.cos(ang), -jnp.sin(ang)], axis=-1)
    emb = z.shape[1]
    emb_pad = -(-emb // SUBLANES) * SUBLANES
    z = jnp.pad(z, ((0, 0), (0, emb_pad - emb)))
    w1 = jnp.pad(w1, ((0, emb_pad - emb), (0, 0)))
    max_decay = math.log(HY_DECAY_TARGET) / HY_FAST_DECAY
    min_decay = math.log(HY_DECAY_TARGET) / HY_SLOW_DECAY
    deltas = jnp.abs(jnp.linspace(min_decay, max_decay, c, dtype=F32))[None, :]
    ffn = w2.shape[0]
    tl = _tile(l, 512, SUBLANES)
    full = lambda shape: pl.BlockSpec(shape, lambda i: (0, 0))
    h, asum, row0 = pl.pallas_call(
        functools.partial(_hy_filter_kernel, seq_len=l),
        out_shape=[jax.ShapeDtypeStruct((l, 2 * c), F32), jax.ShapeDtypeStruct((1, 2 * c), F32),
                   jax.ShapeDtypeStruct((1, 2 * c), F32)],
        grid=(l // tl,),
        in_specs=[pl.BlockSpec((tl, emb_pad), lambda i: (i, 0)), full((emb_pad, ffn)), full((1, ffn)), full((1, ffn)),
                  full((ffn, ffn)), full((1, ffn)), full((1, ffn)), full((ffn, 2 * c)), full((1, c))],
        out_specs=[pl.BlockSpec((tl, 2 * c), lambda i: (i, 0)), full((1, 2 * c)), full((1, 2 * c))],
        compiler_params=_cparams("arbitrary"),
        name="hyena_filter",
    )(z, w1, b1, f1, w2, b2, f2, w3, deltas)
    norm = asum[:, :c] + asum[:, c:] + jnp.abs(row0[:, :c] + row0[:, c:])
    return h, norm


def _dft_tables(n1, n2):
    n = n1 * n2
    n1h = n1 // 2
    th1 = 2.0 * np.pi * np.outer(np.arange(n1), np.arange(n1h)) / n1
    f1 = np.concatenate([np.cos(th1), -np.sin(th1)], axis=0)
    th1i = 2.0 * np.pi * np.outer(np.arange(n1h), np.arange(n1)) / n1
    f1inv_re, f1inv_im = np.cos(th1i), -np.sin(th1i)
    k1 = np.arange(n1)[:, None, None]
    k2 = np.arange(n2)[None, :, None]
    m2 = np.arange(n2)[None, None, :]
    th = 2.0 * np.pi * (m2 * k2 / n2 + m2 * k1 / n)
    gr, gi = np.cos(th), -np.sin(th)
    g = np.concatenate([np.concatenate([gr, -gi], axis=2), np.concatenate([gi, gr], axis=2)], axis=1)
    grt, git = np.swapaxes(gr, 1, 2), np.swapaxes(gi, 1, 2)
    ginv = np.concatenate([np.concatenate([grt, git], axis=2), np.concatenate([-git, grt], axis=2)], axis=1)
    to = lambda a: jnp.asarray(a.astype(np.float32)).astype(BF16)
    return to(f1), (to(f1inv_re), to(f1inv_im)), to(g), to(ginv)


def _lmm_kernel(w_ref, x_ref, o_ref):
    o_ref[...] = _bdot(w_ref[...], x_ref[...]).astype(o_ref.dtype)


def left_matmul(w, x, out_dtype, tc):
    b, k, n = x.shape
    m = w.shape[0]
    tc = _tile(n, tc, LANES)
    return pl.pallas_call(
        _lmm_kernel,
        out_shape=jax.ShapeDtypeStruct((b, m, n), out_dtype),
        grid=(b, n // tc),
        in_specs=[pl.BlockSpec((m, k), lambda bi, j: (0, 0)), pl.BlockSpec((None, k, tc), lambda bi, j: (bi, 0, j))],
        out_specs=pl.BlockSpec((None, m, tc), lambda bi, j: (bi, 0, j)),
        compiler_params=_cparams("parallel", "parallel"),
        name="dft_stage_a",
    )(w, x)


def _dft_mid_fwd_kernel(g_ref, fr_ref, fi_ref, br_ref, bi_ref, hr_ref, hi_ref):
    n2 = fr_ref.shape[0]
    xf = _bdot(g_ref[...], jnp.concatenate([fr_ref[...], fi_ref[...]], axis=0))
    xb = _bdot(g_ref[...], jnp.concatenate([br_ref[...], bi_ref[...]], axis=0))
    hr_ref[...] = xf[:n2] + xb[:n2]
    hi_ref[...] = xf[n2:] - xb[n2:]


def dft_mid_forward(g, a):
    _, _, n1, n2, c2 = a.shape
    c = c2 // 2
    tc = _tile(c, 2048, LANES)
    nct = c // tc
    a_spec = lambda ri, off: pl.BlockSpec((None, None, None, n2, tc), lambda k, j: (0, ri, k, 0, off + j))
    o_spec = pl.BlockSpec((None, n2, tc), lambda k, j: (k, 0, j))
    spec = jax.ShapeDtypeStruct((n1, n2, c), F32)
    return pl.pallas_call(
        _dft_mid_fwd_kernel,
        out_shape=[spec, spec],
        grid=(n1, nct),
        in_specs=[pl.BlockSpec((None, 2 * n2, 2 * n2), lambda k, j: (k, 0, 0)),
                  a_spec(0, 0), a_spec(1, 0), a_spec(0, nct), a_spec(1, nct)],
        out_specs=[o_spec, o_spec],
        compiler_params=_cparams("parallel", "parallel"),
        name="dft_mid_forward",
    )(g, a, a, a, a)


def _dft_mid_kernel(g_ref, gi_ref, ar_ref, ai_ref, hr_ref, hi_ref, br_ref, bi_ref):
    n2 = ar_ref.shape[0]
    x = _bdot(g_ref[...], jnp.concatenate([ar_ref[...], ai_ref[...]], axis=0))
    xr, xi = x[:n2], x[n2:]
    hr, hi = hr_ref[...], hi_ref[...]
    y = jnp.concatenate([xr * hr - xi * hi, xr * hi + xi * hr], axis=0)
    bm = _bdot(gi_ref[...], y)
    br_ref[...] = bm[:n2].astype(br_ref.dtype)
    bi_ref[...] = bm[n2:].astype(bi_ref.dtype)


def dft_mid(g, ginv, a, hr, hi):
    b, _, n1, n2, c = a.shape
    tc = _tile(c, 2048, LANES)
    a_spec = lambda ri: pl.BlockSpec((None, None, None, n2, tc), lambda k, bi, j: (bi, ri, k, 0, j))
    h_spec = pl.BlockSpec((None, n2, tc), lambda k, bi, j: (k, 0, j))
    g_spec = pl.BlockSpec((None, 2 * n2, 2 * n2), lambda k, bi, j: (k, 0, 0))
    o_spec = pl.BlockSpec((None, None, n2, tc), lambda k, bi, j: (bi, k, 0, j))
    out = jax.ShapeDtypeStruct((b, n1, n2, c), BF16)
    return pl.pallas_call(
        _dft_mid_kernel,
        out_shape=[out, out],
        grid=(n1, b, c // tc),
        in_specs=[g_spec, g_spec, a_spec(0), a_spec(1), h_spec, h_spec],
        out_specs=[o_spec, o_spec],
        compiler_params=_cparams("parallel", "parallel", "parallel"),
        name="dft_mid",
    )(g, ginv, a, a, hr, hi)


def _dft_out_kernel(fr_ref, fi_ref, br_ref, bi_ref, x0_ref, t2_ref, sc_ref, o_ref):
    y = _bdot(fr_ref[...], br_ref[...]) + _bdot(fi_ref[...], bi_ref[...])
    o_ref[...] = (x0_ref[...].astype(F32) * (y * sc_ref[...]) + t2_ref[...].astype(F32)).astype(o_ref.dtype)


def dft_out(f1inv, br, bi, x0, t2, scale):
    b, n1, cols = br.shape
    n1h = f1inv[0].shape[0]
    tc = scale.shape[1]
    io = pl.BlockSpec((None, n1h, tc), lambda bi_, j: (bi_, 0, j))
    bspec = pl.BlockSpec((None, n1, tc), lambda bi_, j: (bi_, 0, j))
    fspec = pl.BlockSpec((n1h, n1), lambda bi_, j: (0, 0))
    return pl.pallas_call(
        _dft_out_kernel,
        out_shape=jax.ShapeDtypeStruct((b, n1h, cols), BF16),
        grid=(b, cols // tc),
        in_specs=[fspec, fspec, bspec, bspec, io, io, pl.BlockSpec((1, tc), lambda bi_, j: (0, 0))],
        out_specs=io,
        compiler_params=_cparams("parallel", "parallel"),
        name="dft_stage_a_inverse",
    )(f1inv[0], f1inv[1], br, bi, x0, t2, scale)


def hyena_long_conv(w16, x0, t2, hfilt, norm):
    b, l, c = w16.shape
    n2 = DFT_N2 if (2 * l) % DFT_N2 == 0 and 2 * l // DFT_N2 >= 2 else 2
    n1 = 2 * l // n2
    n1h = n1 // 2
    f1, f1inv, g, ginv = _dft_tables(n1, n2)
    ha = left_matmul(f1, hfilt.astype(BF16).reshape(1, n1h, n2 * 2 * c), BF16, 4096)
    hr, hi = dft_mid_forward(g, ha.reshape(1, 2, n1, n2, 2 * c))
    a = left_matmul(f1, w16.reshape(b, n1h, n2 * c), BF16, 4096)
    br, bi = dft_mid(g, ginv, a.reshape(b, 2, n1, n2, c), hr, hi)
    reps = max(1, 2048 // c)
    scale = jnp.tile(1.0 / (norm * (2.0 * l)), (1, reps))
    y = dft_out(f1inv, br.reshape(b, n1, n2 * c), bi.reshape(b, n1, n2 * c),
                x0.reshape(b, n1h, n2 * c), t2.reshape(b, n1h, n2 * c), scale)
    return y.reshape(b, l, c)


def even_mixer(h, z_w, p, i, q_w):
    b, l, d = h.shape
    c = d - q_w
    in_w = z_w.shape[1]
    kv_w = (in_w - q_w - 3 * c) // 2
    z = matmul(h.reshape(b * l, d), z_w, name="even_in_proj").reshape(b, l, in_w)
    att = banded_attention(z, p["ev_sinks"][i], q_w, kv_w)
    w16, x0, t2 = hyena_prologue(z, p["ev_conv_w"][i], p["ev_conv_b"][i][None], p["ev_hy_bias"][i][None],
                                 q_w + 2 * kv_w, c)
    hfilt, norm = hyena_filter(l, p["ev_filt_w1"][i], p["ev_filt_b1"][i][None], p["ev_filt_f1"][i][None],
                               p["ev_filt_w2"][i], p["ev_filt_b2"][i][None], p["ev_filt_f2"][i][None],
                               p["ev_filt_w3"][i], c)
    y_hy = hyena_long_conv(w16, x0, t2, hfilt, norm)
    return jnp.concatenate([att, y_hy], axis=-1).reshape(b * l, d)


def _rw_mix_kernel(h_ref, p_ref, n_ref, lerp_ref, *o_refs):
    h = h_ref[...]
    prev_row, next_row = _halo_rows(p_ref, n_ref)
    up, dn = _shift_rows(h, prev_row, next_row)
    xx = 0.5 * (up + dn) - h
    for n, o_ref in enumerate(o_refs):
        o_ref[...] = (h + xx * lerp_ref[n:n + 1, :]).astype(o_ref.dtype)


def rwkv_mix(h, lerp):
    b, l, d = h.shape
    n = lerp.shape[0]
    tl = _tile(l, 256, SUBLANES)
    tc = _tile(d, 512, LANES)
    nh = l // SUBLANES
    tps = tl // SUBLANES
    blk = pl.BlockSpec((None, tl, tc), lambda bi, i, j: (bi, i, j))
    return pl.pallas_call(
        _rw_mix_kernel,
        out_shape=[jax.ShapeDtypeStruct((b, l, d), BF16)] * n,
        grid=(b, l // tl, d // tc),
        in_specs=[blk,
                  pl.BlockSpec((None, SUBLANES, tc), lambda bi, i, j: (bi, jnp.maximum(i * tps - 1, 0), j)),
                  pl.BlockSpec((None, SUBLANES, tc), lambda bi, i, j: (bi, jnp.minimum((i + 1) * tps, nh - 1), j)),
                  pl.BlockSpec((n, tc), lambda bi, i, j: (0, j))],
        out_specs=[blk] * n,
        compiler_params=_cparams("parallel", "parallel", "parallel"),
        name="rwkv_mix",
    )(h, h, h, lerp)


def _head_sum(x, m0):
    s0 = jnp.sum(jnp.where(m0, x, 0.0), axis=-1, keepdims=True)
    s1 = jnp.sum(jnp.where(m0, 0.0, x), axis=-1, keepdims=True)
    return jnp.where(m0, s0, s1)


def _wkv_kernel(r_ref, k_ref, v_ref, lw_ref, a_ref, kk_ref, ka_ref, rk_ref, y_ref, bon_ref, s_ref, *, rev):
    @pl.when(pl.program_id(2) == 0)
    def _():
        s_ref[...] = jnp.zeros_like(s_ref)

    pair = 2 * RW_HEAD
    pairs = range(s_ref.shape[0])
    c = r_ref.shape[0]
    c2 = 2 * c
    prep = [_wkv_prep(*(ref[:, g * pair:(g + 1) * pair] for ref in
                        (r_ref, k_ref, v_ref, lw_ref, a_ref, kk_ref, ka_ref, rk_ref)), rev) for g in pairs]
    for g in pairs:
        bon_ref[:, g * pair:(g + 1) * pair] = prep[g]["bonus"]
    state = [s_ref[g] for g in pairs]

    row2 = lax.broadcasted_iota(jnp.int32, (c2, c2), 0)
    col2 = lax.broadcasted_iota(jnp.int32, (c2, c2), 1)
    strict = (col2 > row2) if rev else (col2 < row2)
    incl = (col2 >= row2) if rev else (col2 <= row2)
    m_all = [_bdot_nt(jnp.concatenate([p["aq"], p["rq"]], axis=0), jnp.concatenate([p["bd"], p["kd"]], axis=0))
             for p in prep]
    m_ab = [jnp.where(strict, m[:c2, :c2], 0.0) for m in m_all]
    m_ak = [jnp.where(strict, m[:c2, c2:], 0.0) for m in m_all]
    m_rb = [jnp.where(incl, m[c2:, :c2], 0.0) for m in m_all]
    m_rk = [jnp.where(incl, m[c2:, c2:], 0.0) for m in m_all]

    eye = jnp.where(row2 == col2, 1.0, 0.0)
    x = [eye + m for m in m_ab]
    pw = m_ab
    for _ in range(int(math.log2(c)) - 1):
        pw = [_bdot(p, p) for p in pw]
        x = [xi + _bdot(xi, p) for xi, p in zip(x, pw)]

    ar = [_bdot_nt(jnp.concatenate([p["aq"], p["rq"]], axis=0), s) for p, s in zip(prep, state)]
    mv = [_bdot(jnp.concatenate([mk, mr], axis=0), p["v"]) for p, mk, mr in zip(prep, m_ak, m_rk)]
    part = [a + b for a, b in zip(ar, mv)]
    us = [_bdot(xi, t[:c2]) for xi, t in zip(x, part)]
    ys = [t[c2:] + _bdot(m, u) for t, m, u in zip(part, m_rb, us)]
    upd = [lax.dot_general(jnp.concatenate([p["v"], u], axis=0).astype(BF16),
                           jnp.concatenate([p["kc"], p["bc"]], axis=0).astype(BF16),
                           (((0,), (0,)), ((), ())), preferred_element_type=F32) for p, u in zip(prep, us)]
    for g in pairs:
        y_ref[:, g * pair:(g + 1) * pair] = ys[g][:c] + ys[g][c:]
        s_ref[g] = state[g] * prep[g]["decay"] + upd[g]


def _wkv_prep(r, k, v, lw, a, k_k, k_a, r_k, rev):
    c = r.shape[0]
    m0 = lax.broadcasted_iota(jnp.int32, r.shape, 1) < RW_HEAD
    kk = k * k_k
    kk = kk / jnp.maximum(jnp.sqrt(_head_sum(kk * kk, m0)), 1e-12)
    kd = k * (1.0 + (a - 1.0) * k_a)
    bvec = kk * a

    row = lax.broadcasted_iota(jnp.int32, (c, c), 0)
    col = lax.broadcasted_iota(jnp.int32, (c, c), 1)
    tri = (col >= row) if rev else (col <= row)
    ci = _hdot(tri.astype(F32), lw)
    ctot = jnp.sum(lw, axis=0, keepdims=True)
    inv = jnp.exp(-ci)
    tail = jnp.exp(ctot - ci)

    def stack(x):
        return jnp.concatenate([jnp.where(m0, x, 0.0), jnp.where(m0, 0.0, x)], axis=0)

    return dict(aq=stack(-kk * jnp.exp(ci - lw)), rq=stack(r * jnp.exp(ci)), kd=stack(kd * inv), bd=stack(bvec * inv),
                kc=stack(kd * tail), bc=stack(bvec * tail), v=stack(v), decay=jnp.exp(ctot),
                bonus=_head_sum(r * kd * r_k, m0) * v)


def wkv_scan(r, k, v, lw, a, k_k, k_a, r_k, rev):
    b, l, d = r.shape
    c = RW_CHUNK
    nch = l // c
    pair = 2 * RW_HEAD
    npairs = d // pair
    group = next(g for g in (WKV_PAIRS_PER_STEP, 3, 2, 1) if npairs % g == 0)
    wide = group * pair
    cidx = (lambda ci: nch - 1 - ci) if rev else (lambda ci: ci)
    seq = pl.BlockSpec((None, c, wide), lambda bi, hp, ci: (bi, cidx(ci), hp))
    par = pl.BlockSpec((1, wide), lambda bi, hp, ci: (0, hp))
    out = jax.ShapeDtypeStruct((b, l, d), F32)
    return pl.pallas_call(
        functools.partial(_wkv_kernel, rev=rev),
        out_shape=[out, out],
        grid=(b, npairs // group, nch),
        in_specs=[seq] * 5 + [par] * 3,
        out_specs=[seq, seq],
        scratch_shapes=[pltpu.VMEM((group, pair, pair), F32)],
        compiler_params=_cparams("parallel", "parallel", "arbitrary"),
        name="wkv_scan_bwd" if rev else "wkv_scan_fwd",
    )(r, k, v, lw, a, k_k, k_a, r_k)


def _rw_post_kernel(yf_ref, yb_ref, bf_ref, bb_ref, g_ref, gg_ref, gb_ref, o_ref):
    pair = 2 * RW_HEAD
    m0 = lax.broadcasted_iota(jnp.int32, (yf_ref.shape[0], pair), 1) < RW_HEAD
    for c0 in range(0, yf_ref.shape[1], pair):
        sl = slice(c0, c0 + pair)
        y = yf_ref[:, sl] + yb_ref[:, sl]
        mu = _head_sum(y, m0) * (1.0 / RW_HEAD)
        dlt = y - mu
        var = _head_sum(dlt * dlt, m0) * (1.0 / RW_HEAD)
        yn = dlt * lax.rsqrt(var + RW_GN_EPS) * gg_ref[:, sl] + gb_ref[:, sl] + bf_ref[:, sl] + bb_ref[:, sl]
        o_ref[:, sl] = (yn * g_ref[:, sl]).astype(o_ref.dtype)


def rwkv_post(yf, yb, bf, bb, g, gn_g, gn_b):
    b, l, d = yf.shape
    pair = _tile(d, 1024, 2 * RW_HEAD)
    tl = _tile(l, 256, SUBLANES)
    blk = pl.BlockSpec((None, tl, pair), lambda bi, i, j: (bi, i, j))
    par = pl.BlockSpec((1, pair), lambda bi, i, j: (0, j))
    return pl.pallas_call(
        _rw_post_kernel,
        out_shape=jax.ShapeDtypeStruct((b, l, d), BF16),
        grid=(b, l // tl, d // pair),
        in_specs=[blk] * 5 + [par, par],
        out_specs=blk,
        compiler_params=_cparams("parallel", "parallel", "parallel"),
        name="rwkv_post",
    )(yf, yb, bf, bb, g, gn_g, gn_b)


def rwkv_mixer(h, p, j):
    b, l, d = h.shape
    t = b * l
    xr, xw, xk, xv, xa, xg = (x.reshape(t, d) for x in rwkv_mix(h, p["od_lerp"][j]))
    r = matmul(xr, p["od_w_r"][j], name="rwkv_r")
    k = matmul(xk, p["od_w_k"][j], name="rwkv_k")
    v = matmul(xv, p["od_w_v"][j], name="rwkv_v")
    g = matmul(matmul(xg, p["od_g1"][j], name="rwkv_g1"), p["od_g2"][j], in_act="sigmoid", name="rwkv_g2")
    w1 = jnp.concatenate([p["od_w1"][j][0], p["od_w1"][j][1]], axis=1)
    a1 = jnp.concatenate([p["od_a1"][j][0], p["od_a1"][j][1]], axis=1)
    tw = matmul(xw, w1, name="rwkv_w1")
    ta = matmul(xa, a1, name="rwkv_a1")
    nl = tw.shape[1] // 2
    na = ta.shape[1] // 2
    shp = (b, l, d)
    ys, bons = [], []
    for di, rev in ((0, False), (1, True)):
        lw = matmul(tw[:, di * nl:(di + 1) * nl], p["od_w2"][j][di], bias=p["od_w0"][j][di][None], in_act="tanh",
                    out_act="log_decay", name="rwkv_w2")
        a = matmul(ta[:, di * na:(di + 1) * na], p["od_a2"][j][di], bias=p["od_a0"][j][di][None],
                   out_act="sigmoid", name="rwkv_a2")
        y, bon = wkv_scan(r.reshape(shp), k.reshape(shp), v.reshape(shp), lw.reshape(shp), a.reshape(shp),
                          p["od_k_k"][j][None], p["od_k_a"][j][None], p["od_r_k"][j].reshape(1, d), rev)
        ys.append(y)
        bons.append(bon)
    out = rwkv_post(ys[0], ys[1], bons[0], bons[1], g.reshape(shp), p["od_gn_g"][j][None], p["od_gn_b"][j][None])
    return out.reshape(t, d)


def _top_values(s, k):
    rows = lax.broadcasted_iota(jnp.int32, (k, s.shape[1]), 0)

    def body(i, carry):
        s, vals = carry
        m = jnp.max(s, axis=0, keepdims=True)
        return jnp.where(s >= m, -jnp.inf, s), jnp.where(rows == i, m, vals)

    return lax.fori_loop(0, k, body, (s, jnp.full((k, s.shape[1]), -jnp.inf, F32)))[1]


def _pair_candidates(v1, v2):
    k = v1.shape[0]
    assert k == 2 * SUBLANES
    row = lax.broadcasted_iota(jnp.int32, (SUBLANES, v1.shape[1]), 0)
    groups = [v1[0:1, :] + v2, v1[1:2, :] + v2[0:SUBLANES, :]]
    for i in range(2, SUBLANES):
        groups.append(jnp.where(row < k // (i + 1), v1[i:i + 1, :] + v2[0:SUBLANES, :], -jnp.inf))
    groups.append(v1[SUBLANES:k, :] + v2[0:1, :])
    return jnp.concatenate(groups, axis=0)


def _peer_topk_kernel(q_ref, sub_ref, s_ref, e_ref, tau_ref):
    k = PEER_TOPK
    for h in range(tau_ref.shape[0]):
        s1 = _bdot(sub_ref[2 * h], q_ref[2 * h])
        s2 = _bdot(sub_ref[2 * h + 1], q_ref[2 * h + 1])
        v1 = _top_values(s1, k)
        v2 = _top_values(s2, k)
        top = _top_values(_pair_candidates(v1, v2), k)
        z = jnp.sum(jnp.exp(top - top[0:1, :]), axis=0, keepdims=True)
        s_ref[2 * h] = s1
        s_ref[2 * h + 1] = s2
        e_ref[2 * h] = jnp.exp(s1 - v1[0:1, :])
        e_ref[2 * h + 1] = jnp.exp(s2 - v2[0:1, :]) / z
        tau_ref[h:h + 1, :] = top[k - 1:k, :]


def peer_topk(q_t, sub):
    hp, dk, t = q_t.shape
    nk = sub.shape[1]
    tm = _tile(t, 512, LANES)
    blk = pl.BlockSpec((hp, nk, tm), lambda i: (0, 0, i))
    big = jax.ShapeDtypeStruct((hp, nk, t), F32)
    return pl.pallas_call(
        _peer_topk_kernel,
        out_shape=[big, big, jax.ShapeDtypeStruct((hp // 2, t), F32)],
        grid=(t // tm,),
        in_specs=[pl.BlockSpec((hp, dk, tm), lambda i: (0, 0, i)), pl.BlockSpec((hp, nk, dk), lambda i: (0, 0, 0))],
        out_specs=[blk, blk, pl.BlockSpec((hp // 2, tm), lambda i: (0, i))],
        compiler_params=_cparams("parallel"),
        name="peer_topk",
    )(q_t, sub)


def _peer_gate(s_ref, e_ref, tau_ref, a):
    gate = None
    for h in range(tau_ref.shape[0]):
        s1 = s_ref[2 * h, pl.ds(a, 1), :]
        e1 = e_ref[2 * h, pl.ds(a, 1), :]
        hit = (s1 + s_ref[2 * h + 1]) >= tau_ref[h:h + 1, :]
        w = jnp.where(hit, e1 * e_ref[2 * h + 1], 0.0)
        gate = w if gate is None else gate + w
    return gate


def _gelu(x):
    return 0.5 * x * (1.0 + lax.erf(x * (2.0 ** -0.5)))


def _peer_main_kernel(u_ref, x_ref, v_ref, s_ref, e_ref, tau_ref, o_ref, acc_ref, *, na):
    j = pl.program_id(1)
    nk = s_ref.shape[1]
    ts = na * nk

    @pl.when(j == 0)
    def _():
        acc_ref[...] = jnp.zeros_like(acc_ref)

    acts = [_bdot(u_ref[i * ts:(i + 1) * ts, :], x_ref[...]) for i in range(2)]
    for i in range(2):
        act = _gelu(acts[i])
        pieces = [(_peer_gate(s_ref, e_ref, tau_ref, (2 * j + i) * na + al) * act[al * nk:(al + 1) * nk, :]).astype(BF16)
                  for al in range(na)]
        acc_ref[...] += jnp.dot(v_ref[:, i * ts:(i + 1) * ts], jnp.concatenate(pieces, axis=0),
                                preferred_element_type=F32)

    @pl.when(j == pl.num_programs(1) - 1)
    def _():
        d = o_ref.shape[1]
        step = _tile(d, o_ref.shape[0], LANES)
        for c0 in range(0, d, step):
            o_ref[:, c0:c0 + step] = acc_ref[c0:c0 + step, :].T


def peer_main(u, x_t, v_t, s, e, tau, layer):
    _, ne, d = u.shape
    t = x_t.shape[1]
    nk = s.shape[1]
    tm = _tile(t, 512, LANES)
    na = PEER_KEYS_PER_SUBTILE
    te = 2 * na * nk
    once = pl.Buffered(1)
    tok = lambda shape: pl.BlockSpec(shape, lambda i, j: (0,) * (len(shape) - 1) + (i,), pipeline_mode=once)
    return pl.pallas_call(
        functools.partial(_peer_main_kernel, na=na),
        out_shape=jax.ShapeDtypeStruct((t, d), F32),
        grid=(t // tm, ne // te),
        in_specs=[pl.BlockSpec((None, te, d), lambda i, j: (layer, j, 0)), tok((d, tm)),
                  pl.BlockSpec((None, d, te), lambda i, j: (layer, 0, j)),
                  tok(s.shape[:2] + (tm,)), tok(e.shape[:2] + (tm,)), tok((tau.shape[0], tm))],
        out_specs=pl.BlockSpec((tm, d), lambda i, j: (i, 0), pipeline_mode=once),
        scratch_shapes=[pltpu.VMEM((d, tm), F32)],
        compiler_params=_cparams("parallel", "arbitrary"),
        name="peer_main",
    )(u, x_t, v_t, s, e, tau)


def peer(x_t, wq_t, sub, u, v_t, layer):
    q_t = matmul(wq_t, x_t, layer=layer, name="peer_query")
    hp, nk, dk = sub.shape[1:]
    s, e, tau = peer_topk(q_t.reshape(hp, dk, -1), sub[layer])
    return peer_main(u, x_t, v_t, s, e, tau, layer)


def _prepare(w):
    p = dict(w)
    for name in ("ev_w_in", "ev_w_out", "od_w_r", "od_w_k", "od_w_v", "od_w_o", "od_w1", "od_w2", "od_a1", "od_a2",
                 "od_g1", "od_g2", "pk_u"):
        p[name] = w[name].astype(BF16)
    p["pk_v_t"] = jnp.swapaxes(w["pk_v"].astype(BF16), 1, 2)
    p["pk_wq_t"] = jnp.swapaxes(w["pk_w_q"].astype(BF16), 1, 2)
    sk = w["pk_sub_keys"]
    p["pk_sub"] = sk.astype(BF16).reshape(sk.shape[0], sk.shape[1] * sk.shape[2], sk.shape[3], sk.shape[4])
    return p


def _trunk(x, mods, p):
    b, l, d = x.shape
    t = b * l
    depth = p["ada_w"].shape[0]
    q_w = p["ev_sinks"].shape[1] * ATT_HEAD_DIM
    y = gt2 = None
    for layer in range(depth):
        sh1, sc1, gt1, sh2, sc2, gt2_next = mods[layer]
        even = layer % 2 == 0
        g1 = p["norm_g"][layer, 0][None]
        kw = dict(shift=sh1, scale=sc1, out_dtype=BF16 if even else F32)
        if y is None:
            h = norm_mod(x, g1, **kw)
        else:
            x, h = norm_mod(x, g1, res=y, gate=gt2, emit_x=True, **kw)
        if even:
            pre, w_out = even_mixer(h, p["ev_w_in"][layer // 2], p, layer // 2, q_w), p["ev_w_out"][layer // 2]
        else:
            pre, w_out = rwkv_mixer(h, p, layer // 2), p["od_w_o"][layer // 2]
        x = matmul(pre, w_out, res=x.reshape(t, d), gate=gt1, rows_per_gate=l, name="mixer_out_proj").reshape(b, l, d)
        h2_t = norm_mod(x, p["norm_g"][layer, 1][None], shift=sh2, scale=sc2, transposed=True)
        y = peer(h2_t, p["pk_wq_t"], p["pk_sub"], p["pk_u"], p["pk_v_t"], layer)
        y = y.reshape(b, l, d)
        gt2 = gt2_next
    return norm_mod(x, p["final_g"][None], res=y, gate=gt2, out_dtype=F32)


def kernel(x_prompt, x_sample, c_prompt, c_sample, ada_w, ada_b, norm_g, final_g, ev_w_in, ev_sinks, ev_conv_w, ev_conv_b, ev_filt_w1, ev_filt_b1, ev_filt_f1, ev_filt_w2, ev_filt_b2, ev_filt_f2, ev_filt_w3, ev_hy_bias, ev_w_out, od_lerp, od_w_r, od_w_k, od_w_v, od_w_o, od_w0, od_w1, od_w2, od_a0, od_a1, od_a2, od_g1, od_g2, od_k_k, od_k_a, od_r_k, od_gn_g, od_gn_b, pk_w_q, pk_sub_keys, pk_u, pk_v):
    p = _prepare(dict(
        ada_w=ada_w, ada_b=ada_b, norm_g=norm_g, final_g=final_g, ev_w_in=ev_w_in, ev_sinks=ev_sinks,
        ev_conv_w=ev_conv_w, ev_conv_b=ev_conv_b, ev_filt_w1=ev_filt_w1, ev_filt_b1=ev_filt_b1, ev_filt_f1=ev_filt_f1,
        ev_filt_w2=ev_filt_w2, ev_filt_b2=ev_filt_b2, ev_filt_f2=ev_filt_f2, ev_filt_w3=ev_filt_w3,
        ev_hy_bias=ev_hy_bias, ev_w_out=ev_w_out, od_lerp=od_lerp, od_w_r=od_w_r, od_w_k=od_w_k, od_w_v=od_w_v,
        od_w_o=od_w_o, od_w0=od_w0, od_w1=od_w1, od_w2=od_w2, od_a0=od_a0, od_a1=od_a1, od_a2=od_a2, od_g1=od_g1,
        od_g2=od_g2, od_k_k=od_k_k, od_k_a=od_k_a, od_r_k=od_r_k, od_gn_g=od_gn_g, od_gn_b=od_gn_b,
        pk_w_q=pk_w_q, pk_sub_keys=pk_sub_keys, pk_u=pk_u, pk_v=pk_v))
    depth, d = ada_w.shape[0], ada_w.shape[1]
    bp, bs = c_prompt.shape[0], c_sample.shape[0]
    rows = -(-(bp + bs) // SUBLANES) * SUBLANES
    c_all = jnp.pad(jnp.concatenate([c_prompt, c_sample], axis=0), ((0, rows - bp - bs), (0, 0)))
    mods_p, mods_s = [], []
    for layer in range(depth):
        mod = matmul(c_all, ada_w, bias=ada_b[layer][None], in_act="silu", layer=layer, name="adaln")
        parts = jnp.split(mod, 6, axis=-1)
        mods_p.append([m[:bp, None, :] for m in parts])
        mods_s.append([m[bp:bp + bs, None, :] for m in parts])
    return _trunk(x_prompt, mods_p, p), _trunk(x_sample, mods_s, p)
```

```python
import functools
import math

import jax
import jax.numpy as jnp
import numpy as np
from jax import lax
from jax.experimental import pallas as pl
from jax.experimental.pallas import tpu as pltpu

F32 = jnp.float32
BF16 = jnp.bfloat16

LANES = 128
SUBLANES = 8
VMEM_LIMIT_BYTES = 56 * 1024 * 1024

NORM_EPS = 1e-6
ATT_HEAD_DIM = 128
ATT_GROUP = 4
WINDOW = 128
ROPE_THETA = 10000.0
HY_BANDS = 16
HY_DECAY_TARGET = 1e-2
HY_FAST_DECAY = 0.3
HY_SLOW_DECAY = 1.5
HY_MOD_SHIFT = 0.05
RW_HEAD = 64
RW_GN_EPS = 64e-5
RW_CHUNK = 64
WKV_PAIRS_PER_STEP = 8
PEER_NKEYS = 128
PEER_TOPK = 16
PEER_KEYS_PER_SUBTILE = 2
DFT_N2 = 128
DFT_BLOCK_BYTES = 8 * 1024 * 1024


def _cparams(*sem):
    return pltpu.CompilerParams(dimension_semantics=sem, vmem_limit_bytes=VMEM_LIMIT_BYTES)


def _tile(n, target, mult):
    t = min(n, target)
    t -= t % mult
    while t >= mult:
        if n % t == 0:
            return t
        t -= mult
    return n


def _bdot(a, b):
    return jnp.dot(a.astype(BF16), b.astype(BF16), preferred_element_type=F32)


def _bdot_nt(a, b):
    return lax.dot_general(a.astype(BF16), b.astype(BF16), (((1,), (1,)), ((), ())), preferred_element_type=F32)


def _hdot(a, b):
    return jnp.dot(a.astype(F32), b.astype(F32), preferred_element_type=F32, precision=lax.Precision.HIGHEST)


def _silu(x):
    return x * jax.nn.sigmoid(x)


def _log_decay(z):
    return -jnp.exp(-jax.nn.softplus(-z) - 0.5)


_ACTS = {None: None, "silu": _silu, "tanh": jnp.tanh, "sigmoid": jax.nn.sigmoid, "log_decay": _log_decay}


def _mm_kernel(*refs, in_act, out_act, has_bias, has_res, has_x2):
    x_ref, w_ref = refs[0], refs[1]
    pos = 2
    bias_ref = res_ref = gate_ref = x2_ref = w2_ref = None
    if has_x2:
        x2_ref, w2_ref = refs[pos], refs[pos + 1]
        pos += 2
    if has_bias:
        bias_ref = refs[pos]
        pos += 1
    if has_res:
        res_ref, gate_ref = refs[pos], refs[pos + 1]
        pos += 2
    o_ref = refs[pos]
    x = x_ref[...]
    if in_act is not None:
        x = _ACTS[in_act](x.astype(F32))
    acc = _bdot(x, w_ref[...])
    if has_x2:
        acc = acc + _bdot(x2_ref[...], w2_ref[...])
    if has_bias:
        acc = acc + bias_ref[...]
    if out_act is not None:
        acc = _ACTS[out_act](acc)
    if has_res:
        acc = res_ref[...] + gate_ref[...] * acc
    o_ref[...] = acc.astype(o_ref.dtype)


def matmul(x, w, *, bias=None, in_act=None, out_act=None, res=None, gate=None, rows_per_gate=None,
           out_dtype=F32, tm=1024, tn=512, layer=None, x2=None, w2=None, name="matmul"):
    m, k = x.shape[-2:]
    k2, n = w.shape[-2:]
    assert k == k2
    has_res = res is not None
    tm = _tile(rows_per_gate if has_res else m, tm, SUBLANES)
    tn = _tile(n, tn, LANES)
    assert m % tm == 0
    x_spec = (pl.BlockSpec((tm, k), lambda j, i: (i, 0)) if x.ndim == 2
              else pl.BlockSpec((None, tm, k), lambda j, i: (layer, i, 0)))
    w_spec = (pl.BlockSpec((k, tn), lambda j, i: (0, j)) if w.ndim == 2
              else pl.BlockSpec((None, k, tn), lambda j, i: (layer, 0, j)))
    in_specs = [x_spec, w_spec]
    args = [x, w]
    if x2 is not None:
        in_specs += [pl.BlockSpec((tm, x2.shape[1]), lambda j, i: (i, 0)),
                     pl.BlockSpec((x2.shape[1], tn), lambda j, i: (0, j))]
        args += [x2, w2]
    if bias is not None:
        in_specs.append(pl.BlockSpec((1, tn), lambda j, i: (0, j)))
        args.append(bias)
    if has_res:
        rpg = rows_per_gate // tm
        in_specs.append(pl.BlockSpec((tm, tn), lambda j, i: (i, j)))
        in_specs.append(pl.BlockSpec((None, 1, tn), lambda j, i: (i // rpg, 0, j)))
        args += [res, gate]
    return pl.pallas_call(
        functools.partial(_mm_kernel, in_act=in_act, out_act=out_act, has_bias=bias is not None, has_res=has_res,
                          has_x2=x2 is not None),
        out_shape=jax.ShapeDtypeStruct((m, n), out_dtype),
        grid=(n // tn, m // tm),
        in_specs=in_specs,
        out_specs=pl.BlockSpec((tm, tn), lambda j, i: (i, j)),
        compiler_params=_cparams("parallel", "parallel"),
        name=name,
    )(*args)


def _norm_kernel(*refs, has_res, has_mod, emit_x, transposed):
    pos = 0
    x = refs[pos][...]
    pos += 1
    if has_res:
        x = x + refs[pos + 1][...] * refs[pos][...]
        pos += 2
    g = refs[pos][...]
    pos += 1
    y = x * lax.rsqrt(jnp.mean(x * x, axis=-1, keepdims=True) + NORM_EPS) * g
    if has_mod:
        y = y * (1.0 + refs[pos + 1][...]) + refs[pos][...]
        pos += 2
    if emit_x:
        refs[pos][...] = x
        pos += 1
    if transposed:
        y = y.T
    refs[pos][...] = y.astype(refs[pos].dtype)


def norm_mod(x, g, *, shift=None, scale=None, res=None, gate=None, emit_x=False, out_dtype=BF16, tl=256,
             transposed=False):
    b, l, d = x.shape
    tl = _tile(l, tl, LANES if transposed else SUBLANES)
    nl = l // tl
    row = pl.BlockSpec((None, tl, d), lambda bi, i: (bi, i, 0))
    vec = pl.BlockSpec((None, 1, d), lambda bi, i: (bi, 0, 0))
    in_specs, args = [row], [x]
    if res is not None:
        in_specs += [row, vec]
        args += [res, gate]
    in_specs.append(pl.BlockSpec((1, d), lambda bi, i: (0, 0)))
    args.append(g)
    if shift is not None:
        in_specs += [vec, vec]
        args += [shift, scale]
    if transposed:
        out_shape = [jax.ShapeDtypeStruct((d, b * l), out_dtype)]
        out_specs = [pl.BlockSpec((d, tl), lambda bi, i: (0, bi * nl + i))]
    else:
        out_shape = [jax.ShapeDtypeStruct((b, l, d), out_dtype)]
        out_specs = [row]
    if emit_x:
        out_shape.insert(0, jax.ShapeDtypeStruct((b, l, d), F32))
        out_specs.insert(0, row)
    outs = pl.pallas_call(
        functools.partial(_norm_kernel, has_res=res is not None, has_mod=shift is not None, emit_x=emit_x,
                          transposed=transposed),
        out_shape=out_shape,
        grid=(b, nl),
        in_specs=in_specs,
        out_specs=out_specs,
        compiler_params=_cparams("parallel", "parallel"),
        name="norm_mod",
    )(*args)
    return outs if emit_x else outs[0]


def _rope(x, cos2, sin2):
    return x * cos2 + pltpu.roll(x, ATT_HEAD_DIM // 2, axis=1) * sin2


def _attn_kernel(sink_ref, q_ref, kp_ref, kc_ref, kn_ref, vp_ref, vc_ref, vn_ref,
                 cq_ref, sq_ref, cp_ref, sp_ref, cn_ref, sn_ref, o_ref, *, seq_len):
    n = pl.program_id(1)
    h = pl.program_id(2)
    blk = WINDOW
    k3 = jnp.concatenate([_rope(kp_ref[...], cp_ref[...], sp_ref[...]),
                          _rope(kc_ref[...], cq_ref[...], sq_ref[...]),
                          _rope(kn_ref[...], cn_ref[...], sn_ref[...])], axis=0).astype(BF16)
    v3 = jnp.concatenate([vp_ref[...], vc_ref[...], vn_ref[...]], axis=0).astype(BF16)
    qpos = n * blk + lax.broadcasted_iota(jnp.int32, (blk, 3 * blk), 0)
    kpos = (n - 1) * blk + lax.broadcasted_iota(jnp.int32, (blk, 3 * blk), 1)
    valid = (jnp.abs(qpos - kpos) <= WINDOW) & (kpos >= 0) & (kpos < seq_len)
    heads = range(ATT_GROUP)
    qs = [_rope(q_ref[:, g * ATT_HEAD_DIM:(g + 1) * ATT_HEAD_DIM], cq_ref[...], sq_ref[...]) for g in heads]
    ss = [jnp.where(valid, _bdot_nt(q, k3) * (ATT_HEAD_DIM ** -0.5), -jnp.inf) for q in qs]
    ps = []
    for g in heads:
        sink = sink_ref[h * ATT_GROUP + g]
        m = jnp.maximum(jnp.max(ss[g], axis=-1, keepdims=True), sink)
        p = jnp.exp(ss[g] - m)
        ps.append(p / (jnp.sum(p, axis=-1, keepdims=True) + jnp.exp(sink - m)))
    for g in heads:
        o_ref[:, g * ATT_HEAD_DIM:(g + 1) * ATT_HEAD_DIM] = _bdot(ps[g], v3).astype(o_ref.dtype)


def banded_attention(z, sinks, q_w, kv_w):
    b, l, _ = z.shape
    blk = WINDOW
    nb = l // blk
    hd = ATT_HEAD_DIM
    kvh = kv_w // hd
    gw = ATT_GROUP * hd
    half = hd // 2
    inv = ROPE_THETA ** (-jnp.arange(half, dtype=F32) * 2.0 / hd)
    ang = jnp.arange(l, dtype=F32)[:, None] * inv[None, :]
    cos2 = jnp.concatenate([jnp.cos(ang), jnp.cos(ang)], axis=-1)
    sin2 = jnp.concatenate([-jnp.sin(ang), jnp.sin(ang)], axis=-1)
    kcol, vcol = q_w // hd, (q_w + kv_w) // hd

    def prev(i):
        return jnp.maximum(i - 1, 0)

    def nxt(i):
        return jnp.minimum(i + 1, nb - 1)

    def kv_spec(col0, which):
        return pl.BlockSpec((None, blk, hd), lambda bi, i, h: (bi, which(i), col0 + h))

    def tab_spec(which):
        return pl.BlockSpec((blk, hd), lambda bi, i, h: (which(i), 0))

    same = lambda i: i
    return pl.pallas_call(
        functools.partial(_attn_kernel, seq_len=l),
        out_shape=jax.ShapeDtypeStruct((b, l, q_w), BF16),
        grid=(b, nb, kvh),
        in_specs=[pl.BlockSpec(memory_space=pltpu.SMEM),
                  pl.BlockSpec((None, blk, gw), lambda bi, i, h: (bi, i, h)),
                  kv_spec(kcol, prev), kv_spec(kcol, same), kv_spec(kcol, nxt),
                  kv_spec(vcol, prev), kv_spec(vcol, same), kv_spec(vcol, nxt),
                  tab_spec(same), tab_spec(same), tab_spec(prev), tab_spec(prev), tab_spec(nxt), tab_spec(nxt)],
        out_specs=pl.BlockSpec((None, blk, gw), lambda bi, i, h: (bi, i, h)),
        compiler_params=_cparams("parallel", "parallel", "parallel"),
        name="banded_attention",
    )(sinks, z, z, z, z, z, z, z, cos2, sin2, cos2, sin2, cos2, sin2)


def _shift_rows(x, prev_row, next_row):
    tl = x.shape[0]
    row = lax.broadcasted_iota(jnp.int32, x.shape, 0)
    up = jnp.where(row == 0, prev_row, pltpu.roll(x, 1, axis=0))
    dn = jnp.where(row == tl - 1, next_row, pltpu.roll(x, tl - 1, axis=0))
    return up, dn


def _halo_rows(prev_ref, next_ref):
    i = pl.program_id(1)
    last = pl.num_programs(1) - 1
    prev_row = jnp.where(i == 0, 0.0, prev_ref[SUBLANES - 1:SUBLANES, :])
    next_row = jnp.where(i == last, 0.0, next_ref[0:1, :])
    return prev_row, next_row


def _hy_pro_kernel(*refs):
    groups = [refs[3 * g:3 * g + 3] for g in range(3)]
    cw_refs = refs[9:12]
    cb_refs = refs[12:15]
    bias_ref = refs[15]
    w_ref, x0_ref, t2_ref = refs[16:19]
    u = []
    for (c_ref, p_ref, n_ref), cw_ref, cb_ref in zip(groups, cw_refs, cb_refs):
        x = c_ref[...]
        prev_row, next_row = _halo_rows(p_ref, n_ref)
        up, dn = _shift_rows(x, prev_row, next_row)
        u.append(up * cw_ref[0:1, :] + x * cw_ref[1:2, :] + dn * cw_ref[2:3, :] + cb_ref[...])
    x0, x1, hv = u
    w = hv * x1
    w_ref[...] = w.astype(w_ref.dtype)
    x0_ref[...] = x0.astype(x0_ref.dtype)
    t2_ref[...] = (x0 * (w * bias_ref[...])).astype(t2_ref.dtype)


def hyena_prologue(z, conv_w, conv_b, hy_bias, col0, c):
    b, l, _ = z.shape
    tl = _tile(l, 256, SUBLANES)
    tc = _tile(c, 512, LANES)
    nh = l // SUBLANES
    tps = tl // SUBLANES
    in_specs, args = [], []
    for g in range(3):
        cb0 = (col0 + g * c) // tc
        in_specs += [
            pl.BlockSpec((None, tl, tc), lambda bi, i, j, cb0=cb0: (bi, i, cb0 + j)),
            pl.BlockSpec((None, SUBLANES, tc), lambda bi, i, j, cb0=cb0: (bi, jnp.maximum(i * tps - 1, 0), cb0 + j)),
            pl.BlockSpec((None, SUBLANES, tc), lambda bi, i, j, cb0=cb0: (bi, jnp.minimum((i + 1) * tps, nh - 1), cb0 + j)),
        ]
        args += [z, z, z]
    for g in range(3):
        in_specs.append(pl.BlockSpec((3, tc), lambda bi, i, j, g=g: (0, g * (c // tc) + j)))
        args.append(conv_w)
    for g in range(3):
        in_specs.append(pl.BlockSpec((1, tc), lambda bi, i, j, g=g: (0, g * (c // tc) + j)))
        args.append(conv_b)
    in_specs.append(pl.BlockSpec((1, tc), lambda bi, i, j: (0, j)))
    args.append(hy_bias)
    out_spec = pl.BlockSpec((None, tl, tc), lambda bi, i, j: (bi, i, j))
    return pl.pallas_call(
        _hy_pro_kernel,
        out_shape=[jax.ShapeDtypeStruct((b, l, c), F32)] * 3,
        grid=(b, l // tl, c // tc),
        in_specs=in_specs,
        out_specs=[out_spec] * 3,
        compiler_params=_cparams("parallel", "parallel", "parallel"),
        name="hyena_prologue",
    )(*args)


def _hy_filter_kernel(z_ref, w1_ref, b1_ref, f1_ref, w2_ref, b2_ref, f2_ref, w3_ref, dl_ref,
                      h_ref, asum_ref, row0_ref, *, seq_len):
    i = pl.program_id(0)
    tl = z_ref.shape[0]
    c = dl_ref.shape[1]
    h1 = jnp.sin(f1_ref[...] * (_hdot(z_ref[...], w1_ref[...]) + b1_ref[...]))
    h2 = jnp.sin(f2_ref[...] * (_hdot(h1, w2_ref[...]) + b2_ref[...]))
    h3 = _hdot(h2, w3_ref[...])
    row = i * tl + lax.broadcasted_iota(jnp.int32, (tl, 1), 0)
    t = row.astype(F32) * (1.0 / (seq_len - 1))
    window = jnp.exp(-t * dl_ref[...]) + HY_MOD_SHIFT
    h3 = h3 * jnp.concatenate([window, window], axis=1)
    h_ref[...] = h3

    @pl.when(i == 0)
    def _():
        asum_ref[...] = jnp.zeros_like(asum_ref)
        row0_ref[...] = h3[0:1, :]

    asum_ref[...] += jnp.sum(jnp.where(row == 0, 0.0, jnp.abs(h3)), axis=0, keepdims=True)


def hyena_filter(l, w1, b1, f1, w2, b2, f2, w3, c):
    t = jnp.linspace(0.0, 1.0, l, dtype=F32)[:, None]
    w = 2.0 * math.pi * jnp.arange(l, dtype=F32) / l
    bands = jnp.linspace(1e-4, HY_BANDS - 1, HY_BANDS, dtype=F32)
    ang = w[:, None] * bands[None, :]
    z = jnp.concatenate([t, jnp.cos(ang), -jnp.sin(ang)], axis=-1)
    emb = z.shape[1]
    emb_pad = -(-emb // SUBLANES) * SUBLANES
    z = jnp.pad(z, ((0, 0), (0, emb_pad - emb)))
    w1 = jnp.pad(w1, ((0, emb_pad - emb), (0, 0)))
    max_decay = math.log(HY_DECAY_TARGET) / HY_FAST_DECAY
    min_decay = math.log(HY_DECAY_TARGET) / HY_SLOW_DECAY
    deltas = jnp.abs(jnp.linspace(min_decay, max_decay, c, dtype=F32))[None, :]
    ffn = w2.shape[0]
    tl = _tile(l, 512, SUBLANES)
    full = lambda shape: pl.BlockSpec(shape, lambda i: (0, 0))
    h, asum, row0 = pl.pallas_call(
        functools.partial(_hy_filter_kernel, seq_len=l),
        out_shape=[jax.ShapeDtypeStruct((l, 2 * c), F32), jax.ShapeDtypeStruct((1, 2 * c), F32),
                   jax.ShapeDtypeStruct((1, 2 * c), F32)],
        grid=(l // tl,),
        in_specs=[pl.BlockSpec((tl, emb_pad), lambda i: (i, 0)), full((emb_pad, ffn)), full((1, ffn)), full((1, ffn)),
                  full((ffn, ffn)), full((1, ffn)), full((1, ffn)), full((ffn, 2 * c)), full((1, c))],
        out_specs=[pl.BlockSpec((tl, 2 * c), lambda i: (i, 0)), full((1, 2 * c)), full((1, 2 * c))],
        compiler_params=_cparams("arbitrary"),
        name="hyena_filter",
    )(z, w1, b1, f1, w2, b2, f2, w3, deltas)
    norm = asum[:, :c] + asum[:, c:] + jnp.abs(row0[:, :c] + row0[:, c:])
    return h, norm


def _dft_tables(n1, n2):
    n = n1 * n2
    n1h = n1 // 2
    th1 = 2.0 * np.pi * np.outer(np.arange(n1), np.arange(n1h)) / n1
    f1 = np.concatenate([np.cos(th1), -np.sin(th1)], axis=0)
    th1i = 2.0 * np.pi * np.outer(np.arange(n1h), np.arange(n1)) / n1
    f1inv_re, f1inv_im = np.cos(th1i), -np.sin(th1i)
    k1 = np.arange(n1)[:, None, None]
    k2 = np.arange(n2)[None, :, None]
    m2 = np.arange(n2)[None, None, :]
    th = 2.0 * np.pi * (m2 * k2 / n2 + m2 * k1 / n)
    gr, gi = np.cos(th), -np.sin(th)
    g = np.concatenate([np.concatenate([gr, -gi], axis=2), np.concatenate([gi, gr], axis=2)], axis=1)
    grt, git = np.swapaxes(gr, 1, 2), np.swapaxes(gi, 1, 2)
    ginv = np.concatenate([np.concatenate([grt, git], axis=2), np.concatenate([-git, grt], axis=2)], axis=1)
    to = lambda a: jnp.asarray(a.astype(np.float32)).astype(BF16)
    return to(f1), (to(f1inv_re), to(f1inv_im)), to(g), to(ginv)


def _dft_a_kernel(w_ref, x_ref, o_ref):
    for s in range(x_ref.shape[1]):
        o_ref[:, s, :] = _bdot(w_ref[...], x_ref[:, s, :])


def dft_stage_a(w, x):
    b, k, n2, c = x.shape
    m = w.shape[0]
    tc = _tile(c, max(LANES, DFT_BLOCK_BYTES // (m * SUBLANES * 4)), LANES)
    return pl.pallas_call(
        _dft_a_kernel,
        out_shape=jax.ShapeDtypeStruct((b, m, n2, c), F32),
        grid=(b, n2 // SUBLANES, c // tc),
        in_specs=[pl.BlockSpec((m, k), lambda bi, i, j: (0, 0)),
                  pl.BlockSpec((None, k, SUBLANES, tc), lambda bi, i, j: (bi, 0, i, j))],
        out_specs=pl.BlockSpec((None, m, SUBLANES, tc), lambda bi, i, j: (bi, 0, i, j)),
        compiler_params=_cparams("parallel", "parallel", "parallel"),
        name="dft_stage_a",
    )(w, x)


def _dft_mid_fwd_kernel(g_ref, fr_ref, fi_ref, br_ref, bi_ref, hr_ref, hi_ref):
    n2 = fr_ref.shape[0]
    xf = _bdot(g_ref[...], jnp.concatenate([fr_ref[...], fi_ref[...]], axis=0))
    xb = _bdot(g_ref[...], jnp.concatenate([br_ref[...], bi_ref[...]], axis=0))
    hr_ref[...] = xf[:n2] + xb[:n2]
    hi_ref[...] = xf[n2:] - xb[n2:]


def dft_mid_forward(g, a):
    _, _, n1, n2, c2 = a.shape
    c = c2 // 2
    tc = _tile(c, 2048, LANES)
    nct = c // tc
    a_spec = lambda ri, off: pl.BlockSpec((None, None, None, n2, tc), lambda k, j: (0, ri, k, 0, off + j))
    o_spec = pl.BlockSpec((None, n2, tc), lambda k, j: (k, 0, j))
    spec = jax.ShapeDtypeStruct((n1, n2, c), F32)
    return pl.pallas_call(
        _dft_mid_fwd_kernel,
        out_shape=[spec, spec],
        grid=(n1, nct),
        in_specs=[pl.BlockSpec((None, 2 * n2, 2 * n2), lambda k, j: (k, 0, 0)),
                  a_spec(0, 0), a_spec(1, 0), a_spec(0, nct), a_spec(1, nct)],
        out_specs=[o_spec, o_spec],
        compiler_params=_cparams("parallel", "parallel"),
        name="dft_mid_forward",
    )(g, a, a, a, a)


def _dft_mid_kernel(g_ref, gi_ref, ar_ref, ai_ref, hr_ref, hi_ref, br_ref, bi_ref):
    n2 = ar_ref.shape[0]
    x = _bdot(g_ref[...], jnp.concatenate([ar_ref[...], ai_ref[...]], axis=0))
    xr, xi = x[:n2], x[n2:]
    hr, hi = hr_ref[...], hi_ref[...]
    y = jnp.concatenate([xr * hr - xi * hi, xr * hi + xi * hr], axis=0)
    bm = _bdot(gi_ref[...], y)
    br_ref[...] = bm[:n2].astype(br_ref.dtype)
    bi_ref[...] = bm[n2:].astype(bi_ref.dtype)


def dft_mid(g, ginv, a, hr, hi):
    b, _, n1, n2, c = a.shape
    tc = _tile(c, 2048, LANES)
    a_spec = lambda ri: pl.BlockSpec((None, None, None, n2, tc), lambda k, bi, j: (bi, ri, k, 0, j))
    h_spec = pl.BlockSpec((None, n2, tc), lambda k, bi, j: (k, 0, j))
    g_spec = pl.BlockSpec((None, 2 * n2, 2 * n2), lambda k, bi, j: (k, 0, 0))
    o_spec = pl.BlockSpec((None, None, n2, tc), lambda k, bi, j: (bi, k, 0, j))
    out = jax.ShapeDtypeStruct((b, n1, n2, c), F32)
    return pl.pallas_call(
        _dft_mid_kernel,
        out_shape=[out, out],
        grid=(n1, b, c // tc),
        in_specs=[g_spec, g_spec, a_spec(0), a_spec(1), h_spec, h_spec],
        out_specs=[o_spec, o_spec],
        compiler_params=_cparams("parallel", "parallel", "parallel"),
        name="dft_mid",
    )(g, ginv, a, a, hr, hi)


def _dft_out_kernel(fr_ref, fi_ref, br_ref, bi_ref, x0_ref, t2_ref, sc_ref, o_ref):
    for s in range(o_ref.shape[1]):
        y = _bdot(fr_ref[...], br_ref[:, s, :]) + _bdot(fi_ref[...], bi_ref[:, s, :])
        o_ref[:, s, :] = x0_ref[:, s, :] * (y * sc_ref[...]) + t2_ref[:, s, :]


def dft_out(f1inv, br, bi, x0, t2, scale):
    b, n1, n2, c = br.shape
    n1h = f1inv[0].shape[0]
    tc = _tile(c, max(LANES, DFT_BLOCK_BYTES // (2 * n1 * SUBLANES * 4)), LANES)
    io = pl.BlockSpec((None, n1h, SUBLANES, tc), lambda bi_, i, j: (bi_, 0, i, j))
    bspec = pl.BlockSpec((None, n1, SUBLANES, tc), lambda bi_, i, j: (bi_, 0, i, j))
    fspec = pl.BlockSpec((n1h, n1), lambda bi_, i, j: (0, 0))
    return pl.pallas_call(
        _dft_out_kernel,
        out_shape=jax.ShapeDtypeStruct((b, n1h, n2, c), F32),
        grid=(b, n2 // SUBLANES, c // tc),
        in_specs=[fspec, fspec, bspec, bspec, io, io, pl.BlockSpec((1, tc), lambda bi_, i, j: (0, j))],
        out_specs=io,
        compiler_params=_cparams("parallel", "parallel", "parallel"),
        name="dft_stage_a_inverse",
    )(f1inv[0], f1inv[1], br, bi, x0, t2, scale)


def hyena_long_conv(w, x0, t2, hfilt, norm):
    b, l, c = w.shape
    n2 = DFT_N2
    n1 = 2 * l // n2
    n1h = n1 // 2
    f1, f1inv, g, ginv = _dft_tables(n1, n2)
    ha = dft_stage_a(f1, hfilt.reshape(1, n1h, n2, 2 * c))
    hr, hi = dft_mid_forward(g, ha.reshape(1, 2, n1, n2, 2 * c))
    a = dft_stage_a(f1, w.reshape(b, n1h, n2, c))
    br, bi = dft_mid(g, ginv, a.reshape(b, 2, n1, n2, c), hr, hi)
    scale = 1.0 / (norm * (2.0 * l))
    y = dft_out(f1inv, br, bi, x0.reshape(b, n1h, n2, c), t2.reshape(b, n1h, n2, c), scale)
    return y.reshape(b, l, c)


def even_mixer(h, z_w, p, i, q_w):
    b, l, d = h.shape
    c = d - q_w
    in_w = z_w.shape[1]
    kv_w = (in_w - q_w - 3 * c) // 2
    z = matmul(h.reshape(b * l, d), z_w, name="even_in_proj").reshape(b, l, in_w)
    att = banded_attention(z, p["ev_sinks"][i], q_w, kv_w)
    w16, x0, t2 = hyena_prologue(z, p["ev_conv_w"][i], p["ev_conv_b"][i][None], p["ev_hy_bias"][i][None],
                                 q_w + 2 * kv_w, c)
    hfilt, norm = hyena_filter(l, p["ev_filt_w1"][i], p["ev_filt_b1"][i][None], p["ev_filt_f1"][i][None],
                               p["ev_filt_w2"][i], p["ev_filt_b2"][i][None], p["ev_filt_f2"][i][None],
                               p["ev_filt_w3"][i], c)
    y_hy = hyena_long_conv(w16, x0, t2, hfilt, norm)
    return att.reshape(b * l, q_w), y_hy.reshape(b * l, c)


def _rw_mix_kernel(h_ref, p_ref, n_ref, lerp_ref, *o_refs):
    h = h_ref[...]
    prev_row, next_row = _halo_rows(p_ref, n_ref)
    up, dn = _shift_rows(h, prev_row, next_row)
    xx = 0.5 * (up + dn) - h
    for n, o_ref in enumerate(o_refs):
        o_ref[...] = (h + xx * lerp_ref[n:n + 1, :]).astype(o_ref.dtype)


def rwkv_mix(h, lerp):
    b, l, d = h.shape
    n = lerp.shape[0]
    tl = _tile(l, 256, SUBLANES)
    tc = _tile(d, 512, LANES)
    nh = l // SUBLANES
    tps = tl // SUBLANES
    blk = pl.BlockSpec((None, tl, tc), lambda bi, i, j: (bi, i, j))
    return pl.pallas_call(
        _rw_mix_kernel,
        out_shape=[jax.ShapeDtypeStruct((b, l, d), BF16)] * n,
        grid=(b, l // tl, d // tc),
        in_specs=[blk,
                  pl.BlockSpec((None, SUBLANES, tc), lambda bi, i, j: (bi, jnp.maximum(i * tps - 1, 0), j)),
                  pl.BlockSpec((None, SUBLANES, tc), lambda bi, i, j: (bi, jnp.minimum((i + 1) * tps, nh - 1), j)),
                  pl.BlockSpec((n, tc), lambda bi, i, j: (0, j))],
        out_specs=[blk] * n,
        compiler_params=_cparams("parallel", "parallel", "parallel"),
        name="rwkv_mix",
    )(h, h, h, lerp)


def _head_sum(x, m0):
    s0 = jnp.sum(jnp.where(m0, x, 0.0), axis=-1, keepdims=True)
    s1 = jnp.sum(jnp.where(m0, 0.0, x), axis=-1, keepdims=True)
    return jnp.where(m0, s0, s1)


def _wkv_kernel(r_ref, k_ref, v_ref, lw_ref, a_ref, kk_ref, ka_ref, rk_ref, y_ref, bon_ref, s_ref, *, rev):
    @pl.when(pl.program_id(2) == 0)
    def _():
        s_ref[...] = jnp.zeros_like(s_ref)

    pair = 2 * RW_HEAD
    pairs = range(s_ref.shape[0])
    c = r_ref.shape[0]
    c2 = 2 * c
    prep = [_wkv_prep(*(ref[:, g * pair:(g + 1) * pair] for ref in
                        (r_ref, k_ref, v_ref, lw_ref, a_ref, kk_ref, ka_ref, rk_ref)), rev) for g in pairs]
    for g in pairs:
        bon_ref[:, g * pair:(g + 1) * pair] = prep[g]["bonus"]
    state = [s_ref[g] for g in pairs]

    row2 = lax.broadcasted_iota(jnp.int32, (c2, c2), 0)
    col2 = lax.broadcasted_iota(jnp.int32, (c2, c2), 1)
    strict = (col2 > row2) if rev else (col2 < row2)
    incl = (col2 >= row2) if rev else (col2 <= row2)
    m_all = [_bdot_nt(jnp.concatenate([p["aq"], p["rq"]], axis=0), jnp.concatenate([p["bd"], p["kd"]], axis=0))
             for p in prep]
    m_ab = [jnp.where(strict, m[:c2, :c2], 0.0) for m in m_all]
    m_ak = [jnp.where(strict, m[:c2, c2:], 0.0) for m in m_all]
    m_rb = [jnp.where(incl, m[c2:, :c2], 0.0) for m in m_all]
    m_rk = [jnp.where(incl, m[c2:, c2:], 0.0) for m in m_all]

    eye = jnp.where(row2 == col2, 1.0, 0.0)
    x = [eye + m for m in m_ab]
    pw = m_ab
    for _ in range(int(math.log2(c)) - 1):
        pw = [_bdot(p, p) for p in pw]
        x = [xi + _bdot(xi, p) for xi, p in zip(x, pw)]

    ar = [_bdot_nt(jnp.concatenate([p["aq"], p["rq"]], axis=0), s) for p, s in zip(prep, state)]
    mv = [_bdot(jnp.concatenate([mk, mr], axis=0), p["v"]) for p, mk, mr in zip(prep, m_ak, m_rk)]
    part = [a + b for a, b in zip(ar, mv)]
    us = [_bdot(xi, t[:c2]) for xi, t in zip(x, part)]
    ys = [t[c2:] + _bdot(m, u) for t, m, u in zip(part, m_rb, us)]
    upd = [lax.dot_general(jnp.concatenate([p["v"], u], axis=0).astype(BF16),
                           jnp.concatenate([p["kc"], p["bc"]], axis=0).astype(BF16),
                           (((0,), (0,)), ((), ())), preferred_element_type=F32) for p, u in zip(prep, us)]
    for g in pairs:
        y_ref[:, g * pair:(g + 1) * pair] = ys[g][:c] + ys[g][c:]
        s_ref[g] = state[g] * prep[g]["decay"] + upd[g]


def _wkv_prep(r, k, v, lw, a, k_k, k_a, r_k, rev):
    c = r.shape[0]
    m0 = lax.broadcasted_iota(jnp.int32, r.shape, 1) < RW_HEAD
    kk = k * k_k
    kk = kk / jnp.maximum(jnp.sqrt(_head_sum(kk * kk, m0)), 1e-12)
    kd = k * (1.0 + (a - 1.0) * k_a)
    bvec = kk * a

    row = lax.broadcasted_iota(jnp.int32, (c, c), 0)
    col = lax.broadcasted_iota(jnp.int32, (c, c), 1)
    tri = (col >= row) if rev else (col <= row)
    ci = _hdot(tri.astype(F32), lw)
    ctot = jnp.sum(lw, axis=0, keepdims=True)
    inv = jnp.exp(-ci)
    tail = jnp.exp(ctot - ci)

    def stack(x):
        return jnp.concatenate([jnp.where(m0, x, 0.0), jnp.where(m0, 0.0, x)], axis=0)

    return dict(aq=stack(-kk * jnp.exp(ci - lw)), rq=stack(r * jnp.exp(ci)), kd=stack(kd * inv), bd=stack(bvec * inv),
                kc=stack(kd * tail), bc=stack(bvec * tail), v=stack(v), decay=jnp.exp(ctot),
                bonus=_head_sum(r * kd * r_k, m0) * v)


def wkv_scan(r, k, v, lw, a, k_k, k_a, r_k, rev):
    b, l, d = r.shape
    c = RW_CHUNK
    nch = l // c
    pair = 2 * RW_HEAD
    npairs = d // pair
    group = next(g for g in (WKV_PAIRS_PER_STEP, 3, 2, 1) if npairs % g == 0)
    wide = group * pair
    cidx = (lambda ci: nch - 1 - ci) if rev else (lambda ci: ci)
    seq = pl.BlockSpec((None, c, wide), lambda bi, hp, ci: (bi, cidx(ci), hp))
    par = pl.BlockSpec((1, wide), lambda bi, hp, ci: (0, hp))
    out = jax.ShapeDtypeStruct((b, l, d), F32)
    return pl.pallas_call(
        functools.partial(_wkv_kernel, rev=rev),
        out_shape=[out, out],
        grid=(b, npairs // group, nch),
        in_specs=[seq] * 5 + [par] * 3,
        out_specs=[seq, seq],
        scratch_shapes=[pltpu.VMEM((group, pair, pair), F32)],
        compiler_params=_cparams("parallel", "parallel", "arbitrary"),
        name="wkv_scan_bwd" if rev else "wkv_scan_fwd",
    )(r, k, v, lw, a, k_k, k_a, r_k)


def _rw_post_kernel(yf_ref, yb_ref, bf_ref, bb_ref, g_ref, gg_ref, gb_ref, o_ref):
    pair = 2 * RW_HEAD
    m0 = lax.broadcasted_iota(jnp.int32, (yf_ref.shape[0], pair), 1) < RW_HEAD
    for c0 in range(0, yf_ref.shape[1], pair):
        sl = slice(c0, c0 + pair)
        y = yf_ref[:, sl] + yb_ref[:, sl]
        mu = _head_sum(y, m0) * (1.0 / RW_HEAD)
        dlt = y - mu
        var = _head_sum(dlt * dlt, m0) * (1.0 / RW_HEAD)
        yn = dlt * lax.rsqrt(var + RW_GN_EPS) * gg_ref[:, sl] + gb_ref[:, sl] + bf_ref[:, sl] + bb_ref[:, sl]
        o_ref[:, sl] = (yn * g_ref[:, sl]).astype(o_ref.dtype)


def rwkv_post(yf, yb, bf, bb, g, gn_g, gn_b):
    b, l, d = yf.shape
    pair = _tile(d, 1024, 2 * RW_HEAD)
    tl = _tile(l, 256, SUBLANES)
    blk = pl.BlockSpec((None, tl, pair), lambda bi, i, j: (bi, i, j))
    par = pl.BlockSpec((1, pair), lambda bi, i, j: (0, j))
    return pl.pallas_call(
        _rw_post_kernel,
        out_shape=jax.ShapeDtypeStruct((b, l, d), BF16),
        grid=(b, l // tl, d // pair),
        in_specs=[blk] * 5 + [par, par],
        out_specs=blk,
        compiler_params=_cparams("parallel", "parallel", "parallel"),
        name="rwkv_post",
    )(yf, yb, bf, bb, g, gn_g, gn_b)


def rwkv_mixer(h, p, j):
    b, l, d = h.shape
    t = b * l
    xr, xw, xk, xv, xa, xg = (x.reshape(t, d) for x in rwkv_mix(h, p["od_lerp"][j]))
    r = matmul(xr, p["od_w_r"][j], name="rwkv_r")
    k = matmul(xk, p["od_w_k"][j], name="rwkv_k")
    v = matmul(xv, p["od_w_v"][j], name="rwkv_v")
    g = matmul(matmul(xg, p["od_g1"][j], name="rwkv_g1"), p["od_g2"][j], in_act="sigmoid", name="rwkv_g2")
    w1 = jnp.concatenate([p["od_w1"][j][0], p["od_w1"][j][1]], axis=1)
    a1 = jnp.concatenate([p["od_a1"][j][0], p["od_a1"][j][1]], axis=1)
    tw = matmul(xw, w1, name="rwkv_w1")
    ta = matmul(xa, a1, name="rwkv_a1")
    nl = tw.shape[1] // 2
    na = ta.shape[1] // 2
    shp = (b, l, d)
    ys, bons = [], []
    for di, rev in ((0, False), (1, True)):
        lw = matmul(tw[:, di * nl:(di + 1) * nl], p["od_w2"][j][di], bias=p["od_w0"][j][di][None], in_act="tanh",
                    out_act="log_decay", name="rwkv_w2")
        a = matmul(ta[:, di * na:(di + 1) * na], p["od_a2"][j][di], bias=p["od_a0"][j][di][None],
                   out_act="sigmoid", name="rwkv_a2")
        y, bon = wkv_scan(r.reshape(shp), k.reshape(shp), v.reshape(shp), lw.reshape(shp), a.reshape(shp),
                          p["od_k_k"][j][None], p["od_k_a"][j][None], p["od_r_k"][j].reshape(1, d), rev)
        ys.append(y)
        bons.append(bon)
    out = rwkv_post(ys[0], ys[1], bons[0], bons[1], g.reshape(shp), p["od_gn_g"][j][None], p["od_gn_b"][j][None])
    return out.reshape(t, d)


def _top_values(s, k):
    rows = lax.broadcasted_iota(jnp.int32, (k, s.shape[1]), 0)

    def body(i, carry):
        s, vals = carry
        m = jnp.max(s, axis=0, keepdims=True)
        return jnp.where(s >= m, -jnp.inf, s), jnp.where(rows == i, m, vals)

    return lax.fori_loop(0, k, body, (s, jnp.full((k, s.shape[1]), -jnp.inf, F32)))[1]


def _pair_candidates(v1, v2):
    k = v1.shape[0]
    assert k == 2 * SUBLANES
    row = lax.broadcasted_iota(jnp.int32, (SUBLANES, v1.shape[1]), 0)
    groups = [v1[0:1, :] + v2, v1[1:2, :] + v2[0:SUBLANES, :]]
    for i in range(2, SUBLANES):
        groups.append(jnp.where(row < k // (i + 1), v1[i:i + 1, :] + v2[0:SUBLANES, :], -jnp.inf))
    groups.append(v1[SUBLANES:k, :] + v2[0:1, :])
    return jnp.concatenate(groups, axis=0)


def _peer_topk_kernel(q_ref, sub_ref, s_ref, e_ref, tau_ref):
    k = PEER_TOPK
    for h in range(tau_ref.shape[0]):
        s1 = _bdot(sub_ref[2 * h], q_ref[2 * h])
        s2 = _bdot(sub_ref[2 * h + 1], q_ref[2 * h + 1])
        v1 = _top_values(s1, k)
        v2 = _top_values(s2, k)
        top = _top_values(_pair_candidates(v1, v2), k)
        z = jnp.sum(jnp.exp(top - top[0:1, :]), axis=0, keepdims=True)
        s_ref[2 * h] = s1
        s_ref[2 * h + 1] = s2
        e_ref[2 * h] = jnp.exp(s1 - v1[0:1, :])
        e_ref[2 * h + 1] = jnp.exp(s2 - v2[0:1, :]) / z
        tau_ref[h:h + 1, :] = top[k - 1:k, :]


def peer_topk(q_t, sub):
    hp, dk, t = q_t.shape
    nk = sub.shape[1]
    tm = _tile(t, 512, LANES)
    blk = pl.BlockSpec((hp, nk, tm), lambda i: (0, 0, i))
    big = jax.ShapeDtypeStruct((hp, nk, t), F32)
    return pl.pallas_call(
        _peer_topk_kernel,
        out_shape=[big, big, jax.ShapeDtypeStruct((hp // 2, t), F32)],
        grid=(t // tm,),
        in_specs=[pl.BlockSpec((hp, dk, tm), lambda i: (0, 0, i)), pl.BlockSpec((hp, nk, dk), lambda i: (0, 0, 0))],
        out_specs=[blk, blk, pl.BlockSpec((hp // 2, tm), lambda i: (0, i))],
        compiler_params=_cparams("parallel"),
        name="peer_topk",
    )(q_t, sub)


def _peer_gate(s_ref, e_ref, tau_ref, a):
    gate = None
    for h in range(tau_ref.shape[0]):
        s1 = s_ref[2 * h, pl.ds(a, 1), :]
        e1 = e_ref[2 * h, pl.ds(a, 1), :]
        hit = (s1 + s_ref[2 * h + 1]) >= tau_ref[h:h + 1, :]
        w = jnp.where(hit, e1 * e_ref[2 * h + 1], 0.0)
        gate = w if gate is None else gate + w
    return gate


def _gelu(x):
    return 0.5 * x * (1.0 + lax.erf(x * (2.0 ** -0.5)))


def _peer_main_kernel(u_ref, x_ref, v_ref, s_ref, e_ref, tau_ref, o_ref, acc_ref, *, na):
    j = pl.program_id(1)
    nk = s_ref.shape[1]
    ts = na * nk

    @pl.when(j == 0)
    def _():
        acc_ref[...] = jnp.zeros_like(acc_ref)

    acts = [_bdot(u_ref[i * ts:(i + 1) * ts, :], x_ref[...]) for i in range(2)]
    for i in range(2):
        act = _gelu(acts[i])
        pieces = [(_peer_gate(s_ref, e_ref, tau_ref, (2 * j + i) * na + al) * act[al * nk:(al + 1) * nk, :]).astype(BF16)
                  for al in range(na)]
        acc_ref[...] += jnp.dot(v_ref[:, i * ts:(i + 1) * ts], jnp.concatenate(pieces, axis=0),
                                preferred_element_type=F32)

    @pl.when(j == pl.num_programs(1) - 1)
    def _():
        d = o_ref.shape[1]
        step = _tile(d, o_ref.shape[0], LANES)
        for c0 in range(0, d, step):
            o_ref[:, c0:c0 + step] = acc_ref[c0:c0 + step, :].T


def peer_main(u, x_t, v_t, s, e, tau, layer):
    _, ne, d = u.shape
    t = x_t.shape[1]
    nk = s.shape[1]
    tm = _tile(t, 512, LANES)
    na = PEER_KEYS_PER_SUBTILE
    te = 2 * na * nk
    once = pl.Buffered(1)
    tok = lambda shape: pl.BlockSpec(shape, lambda i, j: (0,) * (len(shape) - 1) + (i,), pipeline_mode=once)
    return pl.pallas_call(
        functools.partial(_peer_main_kernel, na=na),
        out_shape=jax.ShapeDtypeStruct((t, d), F32),
        grid=(t // tm, ne // te),
        in_specs=[pl.BlockSpec((None, te, d), lambda i, j: (layer, j, 0)), tok((d, tm)),
                  pl.BlockSpec((None, d, te), lambda i, j: (layer, 0, j)),
                  tok(s.shape[:2] + (tm,)), tok(e.shape[:2] + (tm,)), tok((tau.shape[0], tm))],
        out_specs=pl.BlockSpec((tm, d), lambda i, j: (i, 0), pipeline_mode=once),
        scratch_shapes=[pltpu.VMEM((d, tm), F32)],
        compiler_params=_cparams("parallel", "arbitrary"),
        name="peer_main",
    )(u, x_t, v_t, s, e, tau)


def peer(x_t, wq_t, sub, u, v_t, layer):
    q_t = matmul(wq_t, x_t, layer=layer, name="peer_query")
    hp, nk, dk = sub.shape[1:]
    s, e, tau = peer_topk(q_t.reshape(hp, dk, -1), sub[layer])
    return peer_main(u, x_t, v_t, s, e, tau, layer)


def _prepare(w):
    p = dict(w)
    for name in ("ev_w_in", "ev_w_out", "od_w_r", "od_w_k", "od_w_v", "od_w_o", "od_w1", "od_w2", "od_a1", "od_a2",
                 "od_g1", "od_g2", "pk_u"):
        p[name] = w[name].astype(BF16)
    p["pk_v_t"] = jnp.swapaxes(w["pk_v"].astype(BF16), 1, 2)
    p["pk_wq_t"] = jnp.swapaxes(w["pk_w_q"].astype(BF16), 1, 2)
    sk = w["pk_sub_keys"]
    p["pk_sub"] = sk.astype(BF16).reshape(sk.shape[0], sk.shape[1] * sk.shape[2], sk.shape[3], sk.shape[4])
    return p


def _trunk(x, mods, p):
    b, l, d = x.shape
    t = b * l
    depth = p["ada_w"].shape[0]
    q_w = p["ev_sinks"].shape[1] * ATT_HEAD_DIM
    y = gt2 = None
    for layer in range(depth):
        sh1, sc1, gt1, sh2, sc2, gt2_next = mods[layer]
        even = layer % 2 == 0
        g1 = p["norm_g"][layer, 0][None]
        kw = dict(shift=sh1, scale=sc1, out_dtype=BF16 if even else F32)
        if y is None:
            h = norm_mod(x, g1, **kw)
        else:
            x, h = norm_mod(x, g1, res=y, gate=gt2, emit_x=True, **kw)
        kw = dict(res=x.reshape(t, d), gate=gt1, rows_per_gate=l, name="mixer_out_proj")
        if even:
            att, hy = even_mixer(h, p["ev_w_in"][layer // 2], p, layer // 2, q_w)
            w_out = p["ev_w_out"][layer // 2]
            x = matmul(att, w_out[:q_w], x2=hy, w2=w_out[q_w:], **kw)
        else:
            x = matmul(rwkv_mixer(h, p, layer // 2), p["od_w_o"][layer // 2], **kw)
        x = x.reshape(b, l, d)
        h2_t = norm_mod(x, p["norm_g"][layer, 1][None], shift=sh2, scale=sc2, transposed=True)
        y = peer(h2_t, p["pk_wq_t"], p["pk_sub"], p["pk_u"], p["pk_v_t"], layer)
        y = y.reshape(b, l, d)
        gt2 = gt2_next
    return norm_mod(x, p["final_g"][None], res=y, gate=gt2, out_dtype=F32)


def kernel(x_prompt, x_sample, c_prompt, c_sample, ada_w, ada_b, norm_g, final_g, ev_w_in, ev_sinks, ev_conv_w, ev_conv_b, ev_filt_w1, ev_filt_b1, ev_filt_f1, ev_filt_w2, ev_filt_b2, ev_filt_f2, ev_filt_w3, ev_hy_bias, ev_w_out, od_lerp, od_w_r, od_w_k, od_w_v, od_w_o, od_w0, od_w1, od_w2, od_a0, od_a1, od_a2, od_g1, od_g2, od_k_k, od_k_a, od_r_k, od_gn_g, od_gn_b, pk_w_q, pk_sub_keys, pk_u, pk_v):
    p = _prepare(dict(
        ada_w=ada_w, ada_b=ada_b, norm_g=norm_g, final_g=final_g, ev_w_in=ev_w_in, ev_sinks=ev_sinks,
        ev_conv_w=ev_conv_w, ev_conv_b=ev_conv_b, ev_filt_w1=ev_filt_w1, ev_filt_b1=ev_filt_b1, ev_filt_f1=ev_filt_f1,
        ev_filt_w2=ev_filt_w2, ev_filt_b2=ev_filt_b2, ev_filt_f2=ev_filt_f2, ev_filt_w3=ev_filt_w3,
        ev_hy_bias=ev_hy_bias, ev_w_out=ev_w_out, od_lerp=od_lerp, od_w_r=od_w_r, od_w_k=od_w_k, od_w_v=od_w_v,
        od_w_o=od_w_o, od_w0=od_w0, od_w1=od_w1, od_w2=od_w2, od_a0=od_a0, od_a1=od_a1, od_a2=od_a2, od_g1=od_g1,
        od_g2=od_g2, od_k_k=od_k_k, od_k_a=od_k_a, od_r_k=od_r_k, od_gn_g=od_gn_g, od_gn_b=od_gn_b,
        pk_w_q=pk_w_q, pk_sub_keys=pk_sub_keys, pk_u=pk_u, pk_v=pk_v))
    depth, d = ada_w.shape[0], ada_w.shape[1]
    bp, bs = c_prompt.shape[0], c_sample.shape[0]
    rows = -(-(bp + bs) // SUBLANES) * SUBLANES
    c_all = jnp.pad(jnp.concatenate([c_prompt, c_sample], axis=0), ((0, rows - bp - bs), (0, 0)))
    mods_p, mods_s = [], []
    for layer in range(depth):
        mod = matmul(c_all, ada_w, bias=ada_b[layer][None], in_act="silu", layer=layer, name="adaln")
        parts = jnp.split(mod, 6, axis=-1)
        mods_p.append([m[:bp, None, :] for m in parts])
        mods_s.append([m[bp:bp + bs, None, :] for m in parts])
    return _trunk(x_prompt, mods_p, p), _trunk(x_sample, mods_s, p)
```

```python
import functools
import math

import jax
import jax.numpy as jnp
import numpy as np
from jax import lax
from jax.experimental import pallas as pl
from jax.experimental.pallas import tpu as pltpu

F32 = jnp.float32
BF16 = jnp.bfloat16

LANES = 128
SUBLANES = 8
VMEM_LIMIT_BYTES = 56 * 1024 * 1024

NORM_EPS = 1e-6
ATT_HEAD_DIM = 128
ATT_GROUP = 4
WINDOW = 128
ROPE_THETA = 10000.0
HY_BANDS = 16
HY_DECAY_TARGET = 1e-2
HY_FAST_DECAY = 0.3
HY_SLOW_DECAY = 1.5
HY_MOD_SHIFT = 0.05
RW_HEAD = 64
RW_GN_EPS = 64e-5
RW_CHUNK = 64
WKV_PAIRS_PER_STEP = 16
PEER_NKEYS = 128
PEER_TOPK = 16
PEER_KEYS_PER_SUBTILE = 2
DFT_N2 = 128
DFT_BLOCK_BYTES = 8 * 1024 * 1024


def _cparams(*sem):
    return pltpu.CompilerParams(dimension_semantics=sem, vmem_limit_bytes=VMEM_LIMIT_BYTES)


def _tile(n, target, mult):
    t = min(n, target)
    t -= t % mult
    while t >= mult:
        if n % t == 0:
            return t
        t -= mult
    return n


def _bdot(a, b):
    return jnp.dot(a.astype(BF16), b.astype(BF16), preferred_element_type=F32)


def _bdot_nt(a, b):
    return lax.dot_general(a.astype(BF16), b.astype(BF16), (((1,), (1,)), ((), ())), preferred_element_type=F32)


def _hdot(a, b):
    return jnp.dot(a.astype(F32), b.astype(F32), preferred_element_type=F32, precision=lax.Precision.HIGHEST)


def _silu(x):
    return x * jax.nn.sigmoid(x)


def _log_decay(z):
    return -jnp.exp(-jax.nn.softplus(-z) - 0.5)


_ACTS = {None: None, "silu": _silu, "tanh": jnp.tanh, "sigmoid": jax.nn.sigmoid, "log_decay": _log_decay}


def _mm_kernel(*refs, in_act, out_act, has_bias, has_res, has_x2):
    x_ref, w_ref = refs[0], refs[1]
    pos = 2
    bias_ref = res_ref = gate_ref = x2_ref = w2_ref = None
    if has_x2:
        x2_ref, w2_ref = refs[pos], refs[pos + 1]
        pos += 2
    if has_bias:
        bias_ref = refs[pos]
        pos += 1
    if has_res:
        res_ref, gate_ref = refs[pos], refs[pos + 1]
        pos += 2
    o_ref = refs[pos]
    x = x_ref[...]
    if in_act is not None:
        x = _ACTS[in_act](x.astype(F32))
    acc = _bdot(x, w_ref[...])
    if has_x2:
        acc = acc + _bdot(x2_ref[...], w2_ref[...])
    if has_bias:
        acc = acc + bias_ref[...]
    if out_act is not None:
        acc = _ACTS[out_act](acc)
    if has_res:
        acc = res_ref[...] + gate_ref[...] * acc
    o_ref[...] = acc.astype(o_ref.dtype)


def matmul(x, w, *, bias=None, in_act=None, out_act=None, res=None, gate=None, rows_per_gate=None,
           out_dtype=F32, tm=1024, tn=512, layer=None, x2=None, w2=None, name="matmul"):
    m, k = x.shape[-2:]
    k2, n = w.shape[-2:]
    assert k == k2
    has_res = res is not None
    tm = _tile(rows_per_gate if has_res else m, tm, SUBLANES)
    tn = _tile(n, tn, LANES)
    assert m % tm == 0
    x_spec = (pl.BlockSpec((tm, k), lambda j, i: (i, 0)) if x.ndim == 2
              else pl.BlockSpec((None, tm, k), lambda j, i: (layer, i, 0)))
    w_spec = (pl.BlockSpec((k, tn), lambda j, i: (0, j)) if w.ndim == 2
              else pl.BlockSpec((None, k, tn), lambda j, i: (layer, 0, j)))
    in_specs = [x_spec, w_spec]
    args = [x, w]
    if x2 is not None:
        in_specs += [pl.BlockSpec((tm, x2.shape[1]), lambda j, i: (i, 0)),
                     pl.BlockSpec((x2.shape[1], tn), lambda j, i: (0, j))]
        args += [x2, w2]
    if bias is not None:
        in_specs.append(pl.BlockSpec((1, tn), lambda j, i: (0, j)))
        args.append(bias)
    if has_res:
        rpg = rows_per_gate // tm
        in_specs.append(pl.BlockSpec((tm, tn), lambda j, i: (i, j)))
        in_specs.append(pl.BlockSpec((None, 1, tn), lambda j, i: (i // rpg, 0, j)))
        args += [res, gate]
    return pl.pallas_call(
        functools.partial(_mm_kernel, in_act=in_act, out_act=out_act, has_bias=bias is not None, has_res=has_res,
                          has_x2=x2 is not None),
        out_shape=jax.ShapeDtypeStruct((m, n), out_dtype),
        grid=(n // tn, m // tm),
        in_specs=in_specs,
        out_specs=pl.BlockSpec((tm, tn), lambda j, i: (i, j)),
        compiler_params=_cparams("parallel", "parallel"),
        name=name,
    )(*args)


def _norm_kernel(*refs, has_res, has_mod, emit_x, transposed):
    pos = 0
    x = refs[pos][...]
    pos += 1
    if has_res:
        x = x + refs[pos + 1][...] * refs[pos][...]
        pos += 2
    g = refs[pos][...]
    pos += 1
    y = x * lax.rsqrt(jnp.mean(x * x, axis=-1, keepdims=True) + NORM_EPS) * g
    if has_mod:
        y = y * (1.0 + refs[pos + 1][...]) + refs[pos][...]
        pos += 2
    if emit_x:
        refs[pos][...] = x
        pos += 1
    if transposed:
        y = y.T
    refs[pos][...] = y.astype(refs[pos].dtype)


def norm_mod(x, g, *, shift=None, scale=None, res=None, gate=None, emit_x=False, out_dtype=BF16, tl=256,
             transposed=False):
    b, l, d = x.shape
    tl = _tile(l, tl, LANES if transposed else SUBLANES)
    nl = l // tl
    row = pl.BlockSpec((None, tl, d), lambda bi, i: (bi, i, 0))
    vec = pl.BlockSpec((None, 1, d), lambda bi, i: (bi, 0, 0))
    in_specs, args = [row], [x]
    if res is not None:
        in_specs += [row, vec]
        args += [res, gate]
    in_specs.append(pl.BlockSpec((1, d), lambda bi, i: (0, 0)))
    args.append(g)
    if shift is not None:
        in_specs += [vec, vec]
        args += [shift, scale]
    if transposed:
        out_shape = [jax.ShapeDtypeStruct((d, b * l), out_dtype)]
        out_specs = [pl.BlockSpec((d, tl), lambda bi, i: (0, bi * nl + i))]
    else:
        out_shape = [jax.ShapeDtypeStruct((b, l, d), out_dtype)]
        out_specs = [row]
    if emit_x:
        out_shape.insert(0, jax.ShapeDtypeStruct((b, l, d), F32))
        out_specs.insert(0, row)
    outs = pl.pallas_call(
        functools.partial(_norm_kernel, has_res=res is not None, has_mod=shift is not None, emit_x=emit_x,
                          transposed=transposed),
        out_shape=out_shape,
        grid=(b, nl),
        in_specs=in_specs,
        out_specs=out_specs,
        compiler_params=_cparams("parallel", "parallel"),
        name="norm_mod",
    )(*args)
    return outs if emit_x else outs[0]


def _rope(x, cos2, sin2):
    return x * cos2 + pltpu.roll(x, ATT_HEAD_DIM // 2, axis=1) * sin2


def _attn_kernel(sink_ref, q_ref, kp_ref, kc_ref, kn_ref, vp_ref, vc_ref, vn_ref,
                 cq_ref, sq_ref, cp_ref, sp_ref, cn_ref, sn_ref, o_ref, *, seq_len):
    n = pl.program_id(1)
    h = pl.program_id(2)
    blk = WINDOW
    k3 = jnp.concatenate([_rope(kp_ref[...], cp_ref[...], sp_ref[...]),
                          _rope(kc_ref[...], cq_ref[...], sq_ref[...]),
                          _rope(kn_ref[...], cn_ref[...], sn_ref[...])], axis=0).astype(BF16)
    v3 = jnp.concatenate([vp_ref[...], vc_ref[...], vn_ref[...]], axis=0).astype(BF16)
    qpos = n * blk + lax.broadcasted_iota(jnp.int32, (blk, 3 * blk), 0)
    kpos = (n - 1) * blk + lax.broadcasted_iota(jnp.int32, (blk, 3 * blk), 1)
    valid = (jnp.abs(qpos - kpos) <= WINDOW) & (kpos >= 0) & (kpos < seq_len)
    heads = range(ATT_GROUP)
    qs = [_rope(q_ref[:, g * ATT_HEAD_DIM:(g + 1) * ATT_HEAD_DIM], cq_ref[...], sq_ref[...]) for g in heads]
    ss = [jnp.where(valid, _bdot_nt(q, k3) * (ATT_HEAD_DIM ** -0.5), -jnp.inf) for q in qs]
    ps = []
    for g in heads:
        sink = sink_ref[h * ATT_GROUP + g]
        m = jnp.maximum(jnp.max(ss[g], axis=-1, keepdims=True), sink)
        p = jnp.exp(ss[g] - m)
        ps.append(p / (jnp.sum(p, axis=-1, keepdims=True) + jnp.exp(sink - m)))
    for g in heads:
        o_ref[:, g * ATT_HEAD_DIM:(g + 1) * ATT_HEAD_DIM] = _bdot(ps[g], v3).astype(o_ref.dtype)


def banded_attention(z, sinks, q_w, kv_w):
    b, l, _ = z.shape
    blk = WINDOW
    nb = l // blk
    hd = ATT_HEAD_DIM
    kvh = kv_w // hd
    gw = ATT_GROUP * hd
    half = hd // 2
    inv = ROPE_THETA ** (-jnp.arange(half, dtype=F32) * 2.0 / hd)
    ang = jnp.arange(l, dtype=F32)[:, None] * inv[None, :]
    cos2 = jnp.concatenate([jnp.cos(ang), jnp.cos(ang)], axis=-1)
    sin2 = jnp.concatenate([-jnp.sin(ang), jnp.sin(ang)], axis=-1)
    kcol, vcol = q_w // hd, (q_w + kv_w) // hd

    def prev(i):
        return jnp.maximum(i - 1, 0)

    def nxt(i):
        return jnp.minimum(i + 1, nb - 1)

    def kv_spec(col0, which):
        return pl.BlockSpec((None, blk, hd), lambda bi, i, h: (bi, which(i), col0 + h))

    def tab_spec(which):
        return pl.BlockSpec((blk, hd), lambda bi, i, h: (which(i), 0))

    same = lambda i: i
    return pl.pallas_call(
        functools.partial(_attn_kernel, seq_len=l),
        out_shape=jax.ShapeDtypeStruct((b, l, q_w), BF16),
        grid=(b, nb, kvh),
        in_specs=[pl.BlockSpec(memory_space=pltpu.SMEM),
                  pl.BlockSpec((None, blk, gw), lambda bi, i, h: (bi, i, h)),
                  kv_spec(kcol, prev), kv_spec(kcol, same), kv_spec(kcol, nxt),
                  kv_spec(vcol, prev), kv_spec(vcol, same), kv_spec(vcol, nxt),
                  tab_spec(same), tab_spec(same), tab_spec(prev), tab_spec(prev), tab_spec(nxt), tab_spec(nxt)],
        out_specs=pl.BlockSpec((None, blk, gw), lambda bi, i, h: (bi, i, h)),
        compiler_params=_cparams("parallel", "parallel", "parallel"),
        name="banded_attention",
    )(sinks, z, z, z, z, z, z, z, cos2, sin2, cos2, sin2, cos2, sin2)


def _shift_rows(x, prev_row, next_row):
    tl = x.shape[0]
    row = lax.broadcasted_iota(jnp.int32, x.shape, 0)
    up = jnp.where(row == 0, prev_row, pltpu.roll(x, 1, axis=0))
    dn = jnp.where(row == tl - 1, next_row, pltpu.roll(x, tl - 1, axis=0))
    return up, dn


def _halo_rows(prev_ref, next_ref):
    i = pl.program_id(1)
    last = pl.num_programs(1) - 1
    prev_row = jnp.where(i == 0, 0.0, prev_ref[SUBLANES - 1:SUBLANES, :])
    next_row = jnp.where(i == last, 0.0, next_ref[0:1, :])
    return prev_row, next_row


def _hy_pro_kernel(*refs):
    groups = [refs[3 * g:3 * g + 3] for g in range(3)]
    cw_refs = refs[9:12]
    cb_refs = refs[12:15]
    bias_ref = refs[15]
    w_ref, x0_ref, t2_ref = refs[16:19]
    u = []
    for (c_ref, p_ref, n_ref), cw_ref, cb_ref in zip(groups, cw_refs, cb_refs):
        x = c_ref[...]
        prev_row, next_row = _halo_rows(p_ref, n_ref)
        up, dn = _shift_rows(x, prev_row, next_row)
        u.append(up * cw_ref[0:1, :] + x * cw_ref[1:2, :] + dn * cw_ref[2:3, :] + cb_ref[...])
    x0, x1, hv = u
    w = hv * x1
    w_ref[...] = w.astype(w_ref.dtype)
    x0_ref[...] = x0.astype(x0_ref.dtype)
    t2_ref[...] = (x0 * (w * bias_ref[...])).astype(t2_ref.dtype)


def hyena_prologue(z, conv_w, conv_b, hy_bias, col0, c):
    b, l, _ = z.shape
    tl = _tile(l, 256, SUBLANES)
    tc = _tile(c, 512, LANES)
    nh = l // SUBLANES
    tps = tl // SUBLANES
    in_specs, args = [], []
    for g in range(3):
        cb0 = (col0 + g * c) // tc
        in_specs += [
            pl.BlockSpec((None, tl, tc), lambda bi, i, j, cb0=cb0: (bi, i, cb0 + j)),
            pl.BlockSpec((None, SUBLANES, tc), lambda bi, i, j, cb0=cb0: (bi, jnp.maximum(i * tps - 1, 0), cb0 + j)),
            pl.BlockSpec((None, SUBLANES, tc), lambda bi, i, j, cb0=cb0: (bi, jnp.minimum((i + 1) * tps, nh - 1), cb0 + j)),
        ]
        args += [z, z, z]
    for g in range(3):
        in_specs.append(pl.BlockSpec((3, tc), lambda bi, i, j, g=g: (0, g * (c // tc) + j)))
        args.append(conv_w)
    for g in range(3):
        in_specs.append(pl.BlockSpec((1, tc), lambda bi, i, j, g=g: (0, g * (c // tc) + j)))
        args.append(conv_b)
    in_specs.append(pl.BlockSpec((1, tc), lambda bi, i, j: (0, j)))
    args.append(hy_bias)
    out_spec = pl.BlockSpec((None, tl, tc), lambda bi, i, j: (bi, i, j))
    return pl.pallas_call(
        _hy_pro_kernel,
        out_shape=[jax.ShapeDtypeStruct((b, l, c), F32)] * 3,
        grid=(b, l // tl, c // tc),
        in_specs=in_specs,
        out_specs=[out_spec] * 3,
        compiler_params=_cparams("parallel", "parallel", "parallel"),
        name="hyena_prologue",
    )(*args)


def _hy_filter_kernel(z_ref, w1_ref, b1_ref, f1_ref, w2_ref, b2_ref, f2_ref, w3_ref, dl_ref,
                      h_ref, asum_ref, row0_ref, *, seq_len):
    i = pl.program_id(0)
    tl = z_ref.shape[0]
    c = dl_ref.shape[1]
    h1 = jnp.sin(f1_ref[...] * (_hdot(z_ref[...], w1_ref[...]) + b1_ref[...]))
    h2 = jnp.sin(f2_ref[...] * (_hdot(h1, w2_ref[...]) + b2_ref[...]))
    h3 = _hdot(h2, w3_ref[...])
    row = i * tl + lax.broadcasted_iota(jnp.int32, (tl, 1), 0)
    t = row.astype(F32) * (1.0 / (seq_len - 1))
    window = jnp.exp(-t * dl_ref[...]) + HY_MOD_SHIFT
    h3 = h3 * jnp.concatenate([window, window], axis=1)
    h_ref[...] = h3

    @pl.when(i == 0)
    def _():
        asum_ref[...] = jnp.zeros_like(asum_ref)
        row0_ref[...] = h3[0:1, :]

    asum_ref[...] += jnp.sum(jnp.where(row == 0, 0.0, jnp.abs(h3)), axis=0, keepdims=True)


def hyena_filter(l, w1, b1, f1, w2, b2, f2, w3, c):
    t = jnp.linspace(0.0, 1.0, l, dtype=F32)[:, None]
    w = 2.0 * math.pi * jnp.arange(l, dtype=F32) / l
    bands = jnp.linspace(1e-4, HY_BANDS - 1, HY_BANDS, dtype=F32)
    ang = w[:, None] * bands[None, :]
    z = jnp.concatenate([t, jnp.cos(ang), -jnp.sin(ang)], axis=-1)
    emb = z.shape[1]
    emb_pad = -(-emb // SUBLANES) * SUBLANES
    z = jnp.pad(z, ((0, 0), (0, emb_pad - emb)))
    w1 = jnp.pad(w1, ((0, emb_pad - emb), (0, 0)))
    max_decay = math.log(HY_DECAY_TARGET) / HY_FAST_DECAY
    min_decay = math.log(HY_DECAY_TARGET) / HY_SLOW_DECAY
    deltas = jnp.abs(jnp.linspace(min_decay, max_decay, c, dtype=F32))[None, :]
    ffn = w2.shape[0]
    tl = _tile(l, 512, SUBLANES)
    full = lambda shape: pl.BlockSpec(shape, lambda i: (0, 0))
    h, asum, row0 = pl.pallas_call(
        functools.partial(_hy_filter_kernel, seq_len=l),
        out_shape=[jax.ShapeDtypeStruct((l, 2 * c), F32), jax.ShapeDtypeStruct((1, 2 * c), F32),
                   jax.ShapeDtypeStruct((1, 2 * c), F32)],
        grid=(l // tl,),
        in_specs=[pl.BlockSpec((tl, emb_pad), lambda i: (i, 0)), full((emb_pad, ffn)), full((1, ffn)), full((1, ffn)),
                  full((ffn, ffn)), full((1, ffn)), full((1, ffn)), full((ffn, 2 * c)), full((1, c))],
        out_specs=[pl.BlockSpec((tl, 2 * c), lambda i: (i, 0)), full((1, 2 * c)), full((1, 2 * c))],
        compiler_params=_cparams("arbitrary"),
        name="hyena_filter",
    )(z, w1, b1, f1, w2, b2, f2, w3, deltas)
    norm = asum[:, :c] + asum[:, c:] + jnp.abs(row0[:, :c] + row0[:, c:])
    return h, norm


def _dft_tables(n1, n2):
    n = n1 * n2
    n1h = n1 // 2
    th1 = 2.0 * np.pi * np.outer(np.arange(n1), np.arange(n1h)) / n1
    f1 = np.concatenate([np.cos(th1), -np.sin(th1)], axis=0)
    th1i = 2.0 * np.pi * np.outer(np.arange(n1h), np.arange(n1)) / n1
    f1inv_re, f1inv_im = np.cos(th1i), -np.sin(th1i)
    k1 = np.arange(n1)[:, None, None]
    k2 = np.arange(n2)[None, :, None]
    m2 = np.arange(n2)[None, None, :]
    th = 2.0 * np.pi * (m2 * k2 / n2 + m2 * k1 / n)
    gr, gi = np.cos(th), -np.sin(th)
    g = np.concatenate([np.concatenate([gr, -gi], axis=2), np.concatenate([gi, gr], axis=2)], axis=1)
    grt, git = np.swapaxes(gr, 1, 2), np.swapaxes(gi, 1, 2)
    ginv = np.concatenate([np.concatenate([grt, git], axis=2), np.concatenate([-git, grt], axis=2)], axis=1)
    to = lambda a: jnp.asarray(a.astype(np.float32)).astype(BF16)
    return to(f1), (to(f1inv_re), to(f1inv_im)), to(g), to(ginv)


def _dft_a_kernel(w_ref, x_ref, o_ref):
    for s in range(x_ref.shape[1]):
        o_ref[:, s, :] = _bdot(w_ref[...], x_ref[:, s, :])


def dft_stage_a(w, x):
    b, k, n2, c = x.shape
    m = w.shape[0]
    tc = _tile(c, max(LANES, DFT_BLOCK_BYTES // (m * SUBLANES * 4)), LANES)
    return pl.pallas_call(
        _dft_a_kernel,
        out_shape=jax.ShapeDtypeStruct((b, m, n2, c), F32),
        grid=(b, n2 // SUBLANES, c // tc),
        in_specs=[pl.BlockSpec((m, k), lambda bi, i, j: (0, 0)),
                  pl.BlockSpec((None, k, SUBLANES, tc), lambda bi, i, j: (bi, 0, i, j))],
        out_specs=pl.BlockSpec((None, m, SUBLANES, tc), lambda bi, i, j: (bi, 0, i, j)),
        compiler_params=_cparams("parallel", "parallel", "parallel"),
        name="dft_stage_a",
    )(w, x)


def _dft_mid_fwd_kernel(g_ref, fr_ref, fi_ref, br_ref, bi_ref, hr_ref, hi_ref):
    n2 = fr_ref.shape[0]
    xf = _bdot(g_ref[...], jnp.concatenate([fr_ref[...], fi_ref[...]], axis=0))
    xb = _bdot(g_ref[...], jnp.concatenate([br_ref[...], bi_ref[...]], axis=0))
    hr_ref[...] = xf[:n2] + xb[:n2]
    hi_ref[...] = xf[n2:] - xb[n2:]


def dft_mid_forward(g, a):
    _, _, n1, n2, c2 = a.shape
    c = c2 // 2
    tc = _tile(c, 2048, LANES)
    nct = c // tc
    a_spec = lambda ri, off: pl.BlockSpec((None, None, None, n2, tc), lambda k, j: (0, ri, k, 0, off + j))
    o_spec = pl.BlockSpec((None, n2, tc), lambda k, j: (k, 0, j))
    spec = jax.ShapeDtypeStruct((n1, n2, c), F32)
    return pl.pallas_call(
        _dft_mid_fwd_kernel,
        out_shape=[spec, spec],
        grid=(n1, nct),
        in_specs=[pl.BlockSpec((None, 2 * n2, 2 * n2), lambda k, j: (k, 0, 0)),
                  a_spec(0, 0), a_spec(1, 0), a_spec(0, nct), a_spec(1, nct)],
        out_specs=[o_spec, o_spec],
        compiler_params=_cparams("parallel", "parallel"),
        name="dft_mid_forward",
    )(g, a, a, a, a)


def _dft_mid_kernel(g_ref, gi_ref, ar_ref, ai_ref, hr_ref, hi_ref, br_ref, bi_ref):
    n2 = ar_ref.shape[0]
    x = _bdot(g_ref[...], jnp.concatenate([ar_ref[...], ai_ref[...]], axis=0))
    xr, xi = x[:n2], x[n2:]
    hr, hi = hr_ref[...], hi_ref[...]
    y = jnp.concatenate([xr * hr - xi * hi, xr * hi + xi * hr], axis=0)
    bm = _bdot(gi_ref[...], y)
    br_ref[...] = bm[:n2].astype(br_ref.dtype)
    bi_ref[...] = bm[n2:].astype(bi_ref.dtype)


def dft_mid(g, ginv, a, hr, hi):
    b, _, n1, n2, c = a.shape
    tc = _tile(c, 2048, LANES)
    a_spec = lambda ri: pl.BlockSpec((None, None, None, n2, tc), lambda k, bi, j: (bi, ri, k, 0, j))
    h_spec = pl.BlockSpec((None, n2, tc), lambda k, bi, j: (k, 0, j))
    g_spec = pl.BlockSpec((None, 2 * n2, 2 * n2), lambda k, bi, j: (k, 0, 0))
    o_spec = pl.BlockSpec((None, None, n2, tc), lambda k, bi, j: (bi, k, 0, j))
    out = jax.ShapeDtypeStruct((b, n1, n2, c), F32)
    return pl.pallas_call(
        _dft_mid_kernel,
        out_shape=[out, out],
        grid=(n1, b, c // tc),
        in_specs=[g_spec, g_spec, a_spec(0), a_spec(1), h_spec, h_spec],
        out_specs=[o_spec, o_spec],
        compiler_params=_cparams("parallel", "parallel", "parallel"),
        name="dft_mid",
    )(g, ginv, a, a, hr, hi)


def _dft_out_kernel(fr_ref, fi_ref, br_ref, bi_ref, x0_ref, t2_ref, sc_ref, o_ref):
    for s in range(o_ref.shape[1]):
        y = _bdot(fr_ref[...], br_ref[:, s, :]) + _bdot(fi_ref[...], bi_ref[:, s, :])
        o_ref[:, s, :] = x0_ref[:, s, :] * (y * sc_ref[...]) + t2_ref[:, s, :]


def dft_out(f1inv, br, bi, x0, t2, scale):
    b, n1, n2, c = br.shape
    n1h = f1inv[0].shape[0]
    tc = _tile(c, max(LANES, DFT_BLOCK_BYTES // (2 * n1 * SUBLANES * 4)), LANES)
    io = pl.BlockSpec((None, n1h, SUBLANES, tc), lambda bi_, i, j: (bi_, 0, i, j))
    bspec = pl.BlockSpec((None, n1, SUBLANES, tc), lambda bi_, i, j: (bi_, 0, i, j))
    fspec = pl.BlockSpec((n1h, n1), lambda bi_, i, j: (0, 0))
    return pl.pallas_call(
        _dft_out_kernel,
        out_shape=jax.ShapeDtypeStruct((b, n1h, n2, c), F32),
        grid=(b, n2 // SUBLANES, c // tc),
        in_specs=[fspec, fspec, bspec, bspec, io, io, pl.BlockSpec((1, tc), lambda bi_, i, j: (0, j))],
        out_specs=io,
        compiler_params=_cparams("parallel", "parallel", "parallel"),
        name="dft_stage_a_inverse",
    )(f1inv[0], f1inv[1], br, bi, x0, t2, scale)


def hyena_long_conv(w, x0, t2, hfilt, norm):
    b, l, c = w.shape
    n2 = DFT_N2
    n1 = 2 * l // n2
    n1h = n1 // 2
    f1, f1inv, g, ginv = _dft_tables(n1, n2)
    ha = dft_stage_a(f1, hfilt.reshape(1, n1h, n2, 2 * c))
    hr, hi = dft_mid_forward(g, ha.reshape(1, 2, n1, n2, 2 * c))
    a = dft_stage_a(f1, w.reshape(b, n1h, n2, c))
    br, bi = dft_mid(g, ginv, a.reshape(b, 2, n1, n2, c), hr, hi)
    scale = 1.0 / (norm * (2.0 * l))
    y = dft_out(f1inv, br, bi, x0.reshape(b, n1h, n2, c), t2.reshape(b, n1h, n2, c), scale)
    return y.reshape(b, l, c)


def even_mixer(h, z_w, p, i, q_w):
    b, l, d = h.shape
    c = d - q_w
    in_w = z_w.shape[1]
    kv_w = (in_w - q_w - 3 * c) // 2
    z = matmul(h.reshape(b * l, d), z_w, name="even_in_proj").reshape(b, l, in_w)
    att = banded_attention(z, p["ev_sinks"][i], q_w, kv_w)
    w16, x0, t2 = hyena_prologue(z, p["ev_conv_w"][i], p["ev_conv_b"][i][None], p["ev_hy_bias"][i][None],
                                 q_w + 2 * kv_w, c)
    hfilt, norm = hyena_filter(l, p["ev_filt_w1"][i], p["ev_filt_b1"][i][None], p["ev_filt_f1"][i][None],
                               p["ev_filt_w2"][i], p["ev_filt_b2"][i][None], p["ev_filt_f2"][i][None],
                               p["ev_filt_w3"][i], c)
    y_hy = hyena_long_conv(w16, x0, t2, hfilt, norm)
    return att.reshape(b * l, q_w), y_hy.reshape(b * l, c)


def _rw_mix_kernel(h_ref, p_ref, n_ref, lerp_ref, *o_refs):
    h = h_ref[...]
    prev_row, next_row = _halo_rows(p_ref, n_ref)
    up, dn = _shift_rows(h, prev_row, next_row)
    xx = 0.5 * (up + dn) - h
    for n, o_ref in enumerate(o_refs):
        o_ref[...] = (h + xx * lerp_ref[n:n + 1, :]).astype(o_ref.dtype)


def rwkv_mix(h, lerp):
    b, l, d = h.shape
    n = lerp.shape[0]
    tl = _tile(l, 256, SUBLANES)
    tc = _tile(d, 512, LANES)
    nh = l // SUBLANES
    tps = tl // SUBLANES
    blk = pl.BlockSpec((None, tl, tc), lambda bi, i, j: (bi, i, j))
    return pl.pallas_call(
        _rw_mix_kernel,
        out_shape=[jax.ShapeDtypeStruct((b, l, d), BF16)] * n,
        grid=(b, l // tl, d // tc),
        in_specs=[blk,
                  pl.BlockSpec((None, SUBLANES, tc), lambda bi, i, j: (bi, jnp.maximum(i * tps - 1, 0), j)),
                  pl.BlockSpec((None, SUBLANES, tc), lambda bi, i, j: (bi, jnp.minimum((i + 1) * tps, nh - 1), j)),
                  pl.BlockSpec((n, tc), lambda bi, i, j: (0, j))],
        out_specs=[blk] * n,
        compiler_params=_cparams("parallel", "parallel", "parallel"),
        name="rwkv_mix",
    )(h, h, h, lerp)


def _head_sum(x, m0):
    s0 = jnp.sum(jnp.where(m0, x, 0.0), axis=-1, keepdims=True)
    s1 = jnp.sum(jnp.where(m0, 0.0, x), axis=-1, keepdims=True)
    return jnp.where(m0, s0, s1)


def _wkv_kernel(r_ref, k_ref, v_ref, lw_ref, a_ref, kk_ref, ka_ref, rk_ref, y_ref, bon_ref, s_ref, *, rev):
    @pl.when(pl.program_id(2) == 0)
    def _():
        s_ref[...] = jnp.zeros_like(s_ref)

    pair = 2 * RW_HEAD
    pairs = range(s_ref.shape[0])
    c = r_ref.shape[0]
    c2 = 2 * c
    prep = [_wkv_prep(*(ref[:, g * pair:(g + 1) * pair] for ref in
                        (r_ref, k_ref, v_ref, lw_ref, a_ref, kk_ref, ka_ref, rk_ref)), rev) for g in pairs]
    for g in pairs:
        bon_ref[:, g * pair:(g + 1) * pair] = prep[g]["bonus"]
    state = [s_ref[g] for g in pairs]

    row2 = lax.broadcasted_iota(jnp.int32, (c2, c2), 0)
    col2 = lax.broadcasted_iota(jnp.int32, (c2, c2), 1)
    strict = (col2 > row2) if rev else (col2 < row2)
    incl = (col2 >= row2) if rev else (col2 <= row2)
    m_all = [_bdot_nt(jnp.concatenate([p["aq"], p["rq"]], axis=0), jnp.concatenate([p["bd"], p["kd"]], axis=0))
             for p in prep]
    m_ab = [jnp.where(strict, m[:c2, :c2], 0.0) for m in m_all]
    m_ak = [jnp.where(strict, m[:c2, c2:], 0.0) for m in m_all]
    m_rb = [jnp.where(incl, m[c2:, :c2], 0.0) for m in m_all]
    m_rk = [jnp.where(incl, m[c2:, c2:], 0.0) for m in m_all]

    eye = jnp.where(row2 == col2, 1.0, 0.0)
    x = [eye + m for m in m_ab]
    pw = m_ab
    for _ in range(int(math.log2(c)) - 1):
        pw = [_bdot(p, p) for p in pw]
        x = [xi + _bdot(xi, p) for xi, p in zip(x, pw)]

    ar = [_bdot_nt(jnp.concatenate([p["aq"], p["rq"]], axis=0), s) for p, s in zip(prep, state)]
    mv = [_bdot(jnp.concatenate([mk, mr], axis=0), p["v"]) for p, mk, mr in zip(prep, m_ak, m_rk)]
    part = [a + b for a, b in zip(ar, mv)]
    us = [_bdot(xi, t[:c2]) for xi, t in zip(x, part)]
    ys = [t[c2:] + _bdot(m, u) for t, m, u in zip(part, m_rb, us)]
    upd = [lax.dot_general(jnp.concatenate([p["v"], u], axis=0).astype(BF16),
                           jnp.concatenate([p["kc"], p["bc"]], axis=0).astype(BF16),
                           (((0,), (0,)), ((), ())), preferred_element_type=F32) for p, u in zip(prep, us)]
    for g in pairs:
        y_ref[:, g * pair:(g + 1) * pair] = ys[g][:c] + ys[g][c:]
        s_ref[g] = state[g] * prep[g]["decay"] + upd[g]


def _wkv_prep(r, k, v, lw, a, k_k, k_a, r_k, rev):
    c = r.shape[0]
    m0 = lax.broadcasted_iota(jnp.int32, r.shape, 1) < RW_HEAD
    kk = k * k_k
    kk = kk / jnp.maximum(jnp.sqrt(_head_sum(kk * kk, m0)), 1e-12)
    kd = k * (1.0 + (a - 1.0) * k_a)
    bvec = kk * a

    row = lax.broadcasted_iota(jnp.int32, (c, c), 0)
    col = lax.broadcasted_iota(jnp.int32, (c, c), 1)
    tri = jnp.where((col >= row) if rev else (col <= row), 1.0, 0.0)
    hi = lw.astype(BF16)
    rest = lw - hi.astype(F32)
    mid = rest.astype(BF16)
    ci = _bdot(tri, hi) + _bdot(tri, mid) + _bdot(tri, rest - mid.astype(F32))
    ctot = jnp.sum(lw, axis=0, keepdims=True)
    inv = jnp.exp(-ci)
    tail = jnp.exp(ctot - ci)

    def stack(x):
        return jnp.concatenate([jnp.where(m0, x, 0.0), jnp.where(m0, 0.0, x)], axis=0)

    return dict(aq=stack(-kk * jnp.exp(ci - lw)), rq=stack(r * jnp.exp(ci)), kd=stack(kd * inv), bd=stack(bvec * inv),
                kc=stack(kd * tail), bc=stack(bvec * tail), v=stack(v), decay=jnp.exp(ctot),
                bonus=_head_sum(r * kd * r_k, m0) * v)


def wkv_scan(r, k, v, lw, a, k_k, k_a, r_k, rev):
    b, l, d = r.shape
    c = RW_CHUNK
    nch = l // c
    pair = 2 * RW_HEAD
    npairs = d // pair
    group = next(g for g in (WKV_PAIRS_PER_STEP, 3, 2, 1) if npairs % g == 0)
    wide = group * pair
    cidx = (lambda ci: nch - 1 - ci) if rev else (lambda ci: ci)
    seq = pl.BlockSpec((None, c, wide), lambda bi, hp, ci: (bi, cidx(ci), hp))
    par = pl.BlockSpec((1, wide), lambda bi, hp, ci: (0, hp))
    out = jax.ShapeDtypeStruct((b, l, d), F32)
    return pl.pallas_call(
        functools.partial(_wkv_kernel, rev=rev),
        out_shape=[out, out],
        grid=(b, npairs // group, nch),
        in_specs=[seq] * 5 + [par] * 3,
        out_specs=[seq, seq],
        scratch_shapes=[pltpu.VMEM((group, pair, pair), F32)],
        compiler_params=_cparams("parallel", "parallel", "arbitrary"),
        name="wkv_scan_bwd" if rev else "wkv_scan_fwd",
    )(r, k, v, lw, a, k_k, k_a, r_k)


def _rw_post_kernel(yf_ref, yb_ref, bf_ref, bb_ref, g_ref, gg_ref, gb_ref, o_ref):
    pair = 2 * RW_HEAD
    m0 = lax.broadcasted_iota(jnp.int32, (yf_ref.shape[0], pair), 1) < RW_HEAD
    for c0 in range(0, yf_ref.shape[1], pair):
        sl = slice(c0, c0 + pair)
        y = yf_ref[:, sl] + yb_ref[:, sl]
        mu = _head_sum(y, m0) * (1.0 / RW_HEAD)
        dlt = y - mu
        var = _head_sum(dlt * dlt, m0) * (1.0 / RW_HEAD)
        yn = dlt * lax.rsqrt(var + RW_GN_EPS) * gg_ref[:, sl] + gb_ref[:, sl] + bf_ref[:, sl] + bb_ref[:, sl]
        o_ref[:, sl] = (yn * g_ref[:, sl]).astype(o_ref.dtype)


def rwkv_post(yf, yb, bf, bb, g, gn_g, gn_b):
    b, l, d = yf.shape
    pair = _tile(d, 1024, 2 * RW_HEAD)
    tl = _tile(l, 256, SUBLANES)
    blk = pl.BlockSpec((None, tl, pair), lambda bi, i, j: (bi, i, j))
    par = pl.BlockSpec((1, pair), lambda bi, i, j: (0, j))
    return pl.pallas_call(
        _rw_post_kernel,
        out_shape=jax.ShapeDtypeStruct((b, l, d), BF16),
        grid=(b, l // tl, d // pair),
        in_specs=[blk] * 5 + [par, par],
        out_specs=blk,
        compiler_params=_cparams("parallel", "parallel", "parallel"),
        name="rwkv_post",
    )(yf, yb, bf, bb, g, gn_g, gn_b)


def rwkv_mixer(h, p, j):
    b, l, d = h.shape
    t = b * l
    xr, xw, xk, xv, xa, xg = (x.reshape(t, d) for x in rwkv_mix(h, p["od_lerp"][j]))
    r = matmul(xr, p["od_w_r"][j], name="rwkv_r")
    k = matmul(xk, p["od_w_k"][j], name="rwkv_k")
    v = matmul(xv, p["od_w_v"][j], name="rwkv_v")
    g = matmul(matmul(xg, p["od_g1"][j], name="rwkv_g1"), p["od_g2"][j], in_act="sigmoid", name="rwkv_g2")
    w1 = jnp.concatenate([p["od_w1"][j][0], p["od_w1"][j][1]], axis=1)
    a1 = jnp.concatenate([p["od_a1"][j][0], p["od_a1"][j][1]], axis=1)
    tw = matmul(xw, w1, name="rwkv_w1")
    ta = matmul(xa, a1, name="rwkv_a1")
    nl = tw.shape[1] // 2
    na = ta.shape[1] // 2
    shp = (b, l, d)
    ys, bons = [], []
    for di, rev in ((0, False), (1, True)):
        lw = matmul(tw[:, di * nl:(di + 1) * nl], p["od_w2"][j][di], bias=p["od_w0"][j][di][None], in_act="tanh",
                    out_act="log_decay", name="rwkv_w2")
        a = matmul(ta[:, di * na:(di + 1) * na], p["od_a2"][j][di], bias=p["od_a0"][j][di][None],
                   out_act="sigmoid", name="rwkv_a2")
        y, bon = wkv_scan(r.reshape(shp), k.reshape(shp), v.reshape(shp), lw.reshape(shp), a.reshape(shp),
                          p["od_k_k"][j][None], p["od_k_a"][j][None], p["od_r_k"][j].reshape(1, d), rev)
        ys.append(y)
        bons.append(bon)
    out = rwkv_post(ys[0], ys[1], bons[0], bons[1], g.reshape(shp), p["od_gn_g"][j][None], p["od_gn_b"][j][None])
    return out.reshape(t, d)


def _top_values(s, k):
    rows = lax.broadcasted_iota(jnp.int32, (k, s.shape[1]), 0)

    def body(i, carry):
        s, vals = carry
        m = jnp.max(s, axis=0, keepdims=True)
        return jnp.where(s >= m, -jnp.inf, s), jnp.where(rows == i, m, vals)

    return lax.fori_loop(0, k, body, (s, jnp.full((k, s.shape[1]), -jnp.inf, F32)))[1]


def _pair_candidates(v1, v2):
    k = v1.shape[0]
    assert k == 2 * SUBLANES
    row = lax.broadcasted_iota(jnp.int32, (SUBLANES, v1.shape[1]), 0)
    groups = [v1[0:1, :] + v2, v1[1:2, :] + v2[0:SUBLANES, :]]
    for i in range(2, SUBLANES):
        groups.append(jnp.where(row < k // (i + 1), v1[i:i + 1, :] + v2[0:SUBLANES, :], -jnp.inf))
    groups.append(v1[SUBLANES:k, :] + v2[0:1, :])
    return jnp.concatenate(groups, axis=0)


def _peer_topk_kernel(q_ref, sub_ref, s_ref, e_ref, tau_ref):
    k = PEER_TOPK
    for h in range(tau_ref.shape[0]):
        s1 = _bdot(sub_ref[2 * h], q_ref[2 * h])
        s2 = _bdot(sub_ref[2 * h + 1], q_ref[2 * h + 1])
        v1 = _top_values(s1, k)
        v2 = _top_values(s2, k)
        top = _top_values(_pair_candidates(v1, v2), k)
        z = jnp.sum(jnp.exp(top - top[0:1, :]), axis=0, keepdims=True)
        s_ref[2 * h] = s1
        s_ref[2 * h + 1] = s2
        e_ref[2 * h] = jnp.exp(s1 - v1[0:1, :])
        e_ref[2 * h + 1] = jnp.exp(s2 - v2[0:1, :]) / z
        tau_ref[h:h + 1, :] = top[k - 1:k, :]


def peer_topk(q_t, sub):
    hp, dk, t = q_t.shape
    nk = sub.shape[1]
    tm = _tile(t, 512, LANES)
    blk = pl.BlockSpec((hp, nk, tm), lambda i: (0, 0, i))
    big = jax.ShapeDtypeStruct((hp, nk, t), F32)
    return pl.pallas_call(
        _peer_topk_kernel,
        out_shape=[big, big, jax.ShapeDtypeStruct((hp // 2, t), F32)],
        grid=(t // tm,),
        in_specs=[pl.BlockSpec((hp, dk, tm), lambda i: (0, 0, i)), pl.BlockSpec((hp, nk, dk), lambda i: (0, 0, 0))],
        out_specs=[blk, blk, pl.BlockSpec((hp // 2, tm), lambda i: (0, i))],
        compiler_params=_cparams("parallel"),
        name="peer_topk",
    )(q_t, sub)


def _peer_gate(s_ref, e_ref, tau_ref, a):
    gate = None
    for h in range(tau_ref.shape[0]):
        s1 = s_ref[2 * h, pl.ds(a, 1), :]
        e1 = e_ref[2 * h, pl.ds(a, 1), :]
        hit = (s1 + s_ref[2 * h + 1]) >= tau_ref[h:h + 1, :]
        w = jnp.where(hit, e1 * e_ref[2 * h + 1], 0.0)
        gate = w if gate is None else gate + w
    return gate


def _gelu(x):
    return 0.5 * x * (1.0 + lax.erf(x * (2.0 ** -0.5)))


def _peer_main_kernel(u_ref, x_ref, v_ref, s_ref, e_ref, tau_ref, o_ref, acc_ref, *, na):
    j = pl.program_id(1)
    nk = s_ref.shape[1]
    ts = na * nk

    @pl.when(j == 0)
    def _():
        acc_ref[...] = jnp.zeros_like(acc_ref)

    acts = [_bdot(u_ref[i * ts:(i + 1) * ts, :], x_ref[...]) for i in range(2)]
    for i in range(2):
        act = _gelu(acts[i])
        pieces = [(_peer_gate(s_ref, e_ref, tau_ref, (2 * j + i) * na + al) * act[al * nk:(al + 1) * nk, :]).astype(BF16)
                  for al in range(na)]
        acc_ref[...] += jnp.dot(v_ref[:, i * ts:(i + 1) * ts], jnp.concatenate(pieces, axis=0),
                                preferred_element_type=F32)

    @pl.when(j == pl.num_programs(1) - 1)
    def _():
        d = o_ref.shape[1]
        step = _tile(d, o_ref.shape[0], LANES)
        for c0 in range(0, d, step):
            o_ref[:, c0:c0 + step] = acc_ref[c0:c0 + step, :].T


def peer_main(u, x_t, v_t, s, e, tau, layer):
    _, ne, d = u.shape
    t = x_t.shape[1]
    nk = s.shape[1]
    tm = _tile(t, 512, LANES)
    na = PEER_KEYS_PER_SUBTILE
    te = 2 * na * nk
    once = pl.Buffered(1)
    tok = lambda shape: pl.BlockSpec(shape, lambda i, j: (0,) * (len(shape) - 1) + (i,), pipeline_mode=once)
    return pl.pallas_call(
        functools.partial(_peer_main_kernel, na=na),
        out_shape=jax.ShapeDtypeStruct((t, d), F32),
        grid=(t // tm, ne // te),
        in_specs=[pl.BlockSpec((None, te, d), lambda i, j: (layer, j, 0)), tok((d, tm)),
                  pl.BlockSpec((None, d, te), lambda i, j: (layer, 0, j)),
                  tok(s.shape[:2] + (tm,)), tok(e.shape[:2] + (tm,)), tok((tau.shape[0], tm))],
        out_specs=pl.BlockSpec((tm, d), lambda i, j: (i, 0), pipeline_mode=once),
        scratch_shapes=[pltpu.VMEM((d, tm), F32)],
        compiler_params=_cparams("parallel", "arbitrary"),
        name="peer_main",
    )(u, x_t, v_t, s, e, tau)


def peer(x_t, wq_t, sub, u, v_t, layer):
    q_t = matmul(wq_t, x_t, layer=layer, name="peer_query")
    hp, nk, dk = sub.shape[1:]
    s, e, tau = peer_topk(q_t.reshape(hp, dk, -1), sub[layer])
    return peer_main(u, x_t, v_t, s, e, tau, layer)


def _prepare(w):
    p = dict(w)
    for name in ("ev_w_in", "ev_w_out", "od_w_r", "od_w_k", "od_w_v", "od_w_o", "od_w1", "od_w2", "od_a1", "od_a2",
                 "od_g1", "od_g2", "pk_u"):
        p[name] = w[name].astype(BF16)
    p["pk_v_t"] = jnp.swapaxes(w["pk_v"].astype(BF16), 1, 2)
    p["pk_wq_t"] = jnp.swapaxes(w["pk_w_q"].astype(BF16), 1, 2)
    sk = w["pk_sub_keys"]
    p["pk_sub"] = sk.astype(BF16).reshape(sk.shape[0], sk.shape[1] * sk.shape[2], sk.shape[3], sk.shape[4])
    return p


def _trunk(x, mods, p):
    b, l, d = x.shape
    t = b * l
    depth = p["ada_w"].shape[0]
    q_w = p["ev_sinks"].shape[1] * ATT_HEAD_DIM
    y = gt2 = None
    for layer in range(depth):
        sh1, sc1, gt1, sh2, sc2, gt2_next = mods[layer]
        even = layer % 2 == 0
        g1 = p["norm_g"][layer, 0][None]
        kw = dict(shift=sh1, scale=sc1, out_dtype=BF16 if even else F32)
        if y is None:
            h = norm_mod(x, g1, **kw)
        else:
            x, h = norm_mod(x, g1, res=y, gate=gt2, emit_x=True, **kw)
        kw = dict(res=x.reshape(t, d), gate=gt1, rows_per_gate=l, name="mixer_out_proj")
        if even:
            att, hy = even_mixer(h, p["ev_w_in"][layer // 2], p, layer // 2, q_w)
            w_out = p["ev_w_out"][layer // 2]
            x = matmul(att, w_out[:q_w], x2=hy, w2=w_out[q_w:], **kw)
        else:
            x = matmul(rwkv_mixer(h, p, layer // 2), p["od_w_o"][layer // 2], **kw)
        x = x.reshape(b, l, d)
        h2_t = norm_mod(x, p["norm_g"][layer, 1][None], shift=sh2, scale=sc2, transposed=True)
        y = peer(h2_t, p["pk_wq_t"], p["pk_sub"], p["pk_u"], p["pk_v_t"], layer)
        y = y.reshape(b, l, d)
        gt2 = gt2_next
    return norm_mod(x, p["final_g"][None], res=y, gate=gt2, out_dtype=F32)


def kernel(x_prompt, x_sample, c_prompt, c_sample, ada_w, ada_b, norm_g, final_g, ev_w_in, ev_sinks, ev_conv_w, ev_conv_b, ev_filt_w1, ev_filt_b1, ev_filt_f1, ev_filt_w2, ev_filt_b2, ev_filt_f2, ev_filt_w3, ev_hy_bias, ev_w_out, od_lerp, od_w_r, od_w_k, od_w_v, od_w_o, od_w0, od_w1, od_w2, od_a0, od_a1, od_a2, od_g1, od_g2, od_k_k, od_k_a, od_r_k, od_gn_g, od_gn_b, pk_w_q, pk_sub_keys, pk_u, pk_v):
    p = _prepare(dict(
        ada_w=ada_w, ada_b=ada_b, norm_g=norm_g, final_g=final_g, ev_w_in=ev_w_in, ev_sinks=ev_sinks,
        ev_conv_w=ev_conv_w, ev_conv_b=ev_conv_b, ev_filt_w1=ev_filt_w1, ev_filt_b1=ev_filt_b1, ev_filt_f1=ev_filt_f1,
        ev_filt_w2=ev_filt_w2, ev_filt_b2=ev_filt_b2, ev_filt_f2=ev_filt_f2, ev_filt_w3=ev_filt_w3,
        ev_hy_bias=ev_hy_bias, ev_w_out=ev_w_out, od_lerp=od_lerp, od_w_r=od_w_r, od_w_k=od_w_k, od_w_v=od_w_v,
        od_w_o=od_w_o, od_w0=od_w0, od_w1=od_w1, od_w2=od_w2, od_a0=od_a0, od_a1=od_a1, od_a2=od_a2, od_g1=od_g1,
        od_g2=od_g2, od_k_k=od_k_k, od_k_a=od_k_a, od_r_k=od_r_k, od_gn_g=od_gn_g, od_gn_b=od_gn_b,
        pk_w_q=pk_w_q, pk_sub_keys=pk_sub_keys, pk_u=pk_u, pk_v=pk_v))
    depth, d = ada_w.shape[0], ada_w.shape[1]
    bp, bs = c_prompt.shape[0], c_sample.shape[0]
    rows = -(-(bp + bs) // SUBLANES) * SUBLANES
    c_all = jnp.pad(jnp.concatenate([c_prompt, c_sample], axis=0), ((0, rows - bp - bs), (0, 0)))
    mods_p, mods_s = [], []
    for layer in range(depth):
        mod = matmul(c_all, ada_w, bias=ada_b[layer][None], in_act="silu", layer=layer, name="adaln")
        parts = jnp.split(mod, 6, axis=-1)
        mods_p.append([m[:bp, None, :] for m in parts])
        mods_s.append([m[bp:bp + bs, None, :] for m in parts])
    return _trunk(x_prompt, mods_p, p), _trunk(x_sample, mods_s, p)
```

```python
import functools
import math

import jax
import jax.numpy as jnp
import numpy as np
from jax import lax
from jax.experimental import pallas as pl
from jax.experimental.pallas import tpu as pltpu

F32 = jnp.float32
BF16 = jnp.bfloat16

LANES = 128
SUBLANES = 8
VMEM_LIMIT_BYTES = 56 * 1024 * 1024

NORM_EPS = 1e-6
ATT_HEAD_DIM = 128
ATT_GROUP = 4
WINDOW = 128
ROPE_THETA = 10000.0
HY_BANDS = 16
HY_DECAY_TARGET = 1e-2
HY_FAST_DECAY = 0.3
HY_SLOW_DECAY = 1.5
HY_MOD_SHIFT = 0.05
RW_HEAD = 64
RW_GN_EPS = 64e-5
RW_CHUNK = 64
WKV_PAIRS_PER_STEP = 16
PEER_NKEYS = 128
PEER_TOPK = 16
PEER_KEYS_PER_SUBTILE = 2
DFT_N2 = 128
DFT_BLOCK_BYTES = 8 * 1024 * 1024


def _cparams(*sem):
    return pltpu.CompilerParams(dimension_semantics=sem, vmem_limit_bytes=VMEM_LIMIT_BYTES)


def _tile(n, target, mult):
    t = min(n, target)
    t -= t % mult
    while t >= mult:
        if n % t == 0:
            return t
        t -= mult
    return n


def _bdot(a, b):
    return jnp.dot(a.astype(BF16), b.astype(BF16), preferred_element_type=F32)


def _bdot_nt(a, b):
    return lax.dot_general(a.astype(BF16), b.astype(BF16), (((1,), (1,)), ((), ())), preferred_element_type=F32)


def _hdot(a, b):
    return jnp.dot(a.astype(F32), b.astype(F32), preferred_element_type=F32, precision=lax.Precision.HIGHEST)


def _silu(x):
    return x * jax.nn.sigmoid(x)


def _log_decay(z):
    return -jnp.exp(-jax.nn.softplus(-z) - 0.5)


_ACTS = {None: None, "silu": _silu, "tanh": jnp.tanh, "sigmoid": jax.nn.sigmoid, "log_decay": _log_decay}


def _mm_kernel(*refs, in_act, out_act, has_bias, has_res, has_x2):
    x_ref, w_ref = refs[0], refs[1]
    pos = 2
    bias_ref = res_ref = gate_ref = x2_ref = w2_ref = None
    if has_x2:
        x2_ref, w2_ref = refs[pos], refs[pos + 1]
        pos += 2
    if has_bias:
        bias_ref = refs[pos]
        pos += 1
    if has_res:
        res_ref, gate_ref = refs[pos], refs[pos + 1]
        pos += 2
    o_ref = refs[pos]
    x = x_ref[...]
    if in_act is not None:
        x = _ACTS[in_act](x.astype(F32))
    acc = _bdot(x, w_ref[...])
    if has_x2:
        acc = acc + _bdot(x2_ref[...], w2_ref[...])
    if has_bias:
        acc = acc + bias_ref[...]
    if out_act is not None:
        acc = _ACTS[out_act](acc)
    if has_res:
        acc = res_ref[...] + gate_ref[...] * acc
    o_ref[...] = acc.astype(o_ref.dtype)


def matmul(x, w, *, bias=None, in_act=None, out_act=None, res=None, gate=None, rows_per_gate=None,
           out_dtype=F32, tm=1024, tn=512, layer=None, x2=None, w2=None, name="matmul"):
    m, k = x.shape[-2:]
    k2, n = w.shape[-2:]
    assert k == k2
    has_res = res is not None
    tm = _tile(rows_per_gate if has_res else m, tm, SUBLANES)
    tn = _tile(n, tn, LANES)
    assert m % tm == 0
    x_spec = (pl.BlockSpec((tm, k), lambda j, i: (i, 0)) if x.ndim == 2
              else pl.BlockSpec((None, tm, k), lambda j, i: (layer, i, 0)))
    w_spec = (pl.BlockSpec((k, tn), lambda j, i: (0, j)) if w.ndim == 2
              else pl.BlockSpec((None, k, tn), lambda j, i: (layer, 0, j)))
    in_specs = [x_spec, w_spec]
    args = [x, w]
    if x2 is not None:
        in_specs += [pl.BlockSpec((tm, x2.shape[1]), lambda j, i: (i, 0)),
                     pl.BlockSpec((x2.shape[1], tn), lambda j, i: (0, j))]
        args += [x2, w2]
    if bias is not None:
        in_specs.append(pl.BlockSpec((1, tn), lambda j, i: (0, j)))
        args.append(bias)
    if has_res:
        rpg = rows_per_gate // tm
        in_specs.append(pl.BlockSpec((tm, tn), lambda j, i: (i, j)))
        in_specs.append(pl.BlockSpec((None, 1, tn), lambda j, i: (i // rpg, 0, j)))
        args += [res, gate]
    return pl.pallas_call(
        functools.partial(_mm_kernel, in_act=in_act, out_act=out_act, has_bias=bias is not None, has_res=has_res,
                          has_x2=x2 is not None),
        out_shape=jax.ShapeDtypeStruct((m, n), out_dtype),
        grid=(n // tn, m // tm),
        in_specs=in_specs,
        out_specs=pl.BlockSpec((tm, tn), lambda j, i: (i, j)),
        compiler_params=_cparams("parallel", "parallel"),
        name=name,
    )(*args)


def _norm_kernel(*refs, has_res, has_mod, emit_x, transposed):
    pos = 0
    x = refs[pos][...]
    pos += 1
    if has_res:
        x = x + refs[pos + 1][...] * refs[pos][...]
        pos += 2
    g = refs[pos][...]
    pos += 1
    y = x * lax.rsqrt(jnp.mean(x * x, axis=-1, keepdims=True) + NORM_EPS) * g
    if has_mod:
        y = y * (1.0 + refs[pos + 1][...]) + refs[pos][...]
        pos += 2
    if emit_x:
        refs[pos][...] = x
        pos += 1
    if transposed:
        y = y.T
    refs[pos][...] = y.astype(refs[pos].dtype)


def norm_mod(x, g, *, shift=None, scale=None, res=None, gate=None, emit_x=False, out_dtype=BF16, tl=256,
             transposed=False):
    b, l, d = x.shape
    tl = _tile(l, tl, LANES if transposed else SUBLANES)
    nl = l // tl
    row = pl.BlockSpec((None, tl, d), lambda bi, i: (bi, i, 0))
    vec = pl.BlockSpec((None, 1, d), lambda bi, i: (bi, 0, 0))
    in_specs, args = [row], [x]
    if res is not None:
        in_specs += [row, vec]
        args += [res, gate]
    in_specs.append(pl.BlockSpec((1, d), lambda bi, i: (0, 0)))
    args.append(g)
    if shift is not None:
        in_specs += [vec, vec]
        args += [shift, scale]
    if transposed:
        out_shape = [jax.ShapeDtypeStruct((d, b * l), out_dtype)]
        out_specs = [pl.BlockSpec((d, tl), lambda bi, i: (0, bi * nl + i))]
    else:
        out_shape = [jax.ShapeDtypeStruct((b, l, d), out_dtype)]
        out_specs = [row]
    if emit_x:
        out_shape.insert(0, jax.ShapeDtypeStruct((b, l, d), F32))
        out_specs.insert(0, row)
    outs = pl.pallas_call(
        functools.partial(_norm_kernel, has_res=res is not None, has_mod=shift is not None, emit_x=emit_x,
                          transposed=transposed),
        out_shape=out_shape,
        grid=(b, nl),
        in_specs=in_specs,
        out_specs=out_specs,
        compiler_params=_cparams("parallel", "parallel"),
        name="norm_mod",
    )(*args)
    return outs if emit_x else outs[0]


def _rope(x, cos2, sin2):
    return x * cos2 + pltpu.roll(x, ATT_HEAD_DIM // 2, axis=1) * sin2


def _attn_kernel(sink_ref, q_ref, kp_ref, kc_ref, kn_ref, vp_ref, vc_ref, vn_ref,
                 cq_ref, sq_ref, cp_ref, sp_ref, cn_ref, sn_ref, o_ref, *, seq_len):
    n = pl.program_id(1)
    h = pl.program_id(2)
    blk = WINDOW
    k3 = jnp.concatenate([_rope(kp_ref[...], cp_ref[...], sp_ref[...]),
                          _rope(kc_ref[...], cq_ref[...], sq_ref[...]),
                          _rope(kn_ref[...], cn_ref[...], sn_ref[...])], axis=0).astype(BF16)
    v3 = jnp.concatenate([vp_ref[...], vc_ref[...], vn_ref[...]], axis=0).astype(BF16)
    qpos = n * blk + lax.broadcasted_iota(jnp.int32, (blk, 3 * blk), 0)
    kpos = (n - 1) * blk + lax.broadcasted_iota(jnp.int32, (blk, 3 * blk), 1)
    valid = (jnp.abs(qpos - kpos) <= WINDOW) & (kpos >= 0) & (kpos < seq_len)
    heads = range(ATT_GROUP)
    qs = [_rope(q_ref[:, g * ATT_HEAD_DIM:(g + 1) * ATT_HEAD_DIM], cq_ref[...], sq_ref[...]) for g in heads]
    ss = [jnp.where(valid, _bdot_nt(q, k3) * (ATT_HEAD_DIM ** -0.5), -jnp.inf) for q in qs]
    ps = []
    for g in heads:
        sink = sink_ref[h * ATT_GROUP + g]
        m = jnp.maximum(jnp.max(ss[g], axis=-1, keepdims=True), sink)
        p = jnp.exp(ss[g] - m)
        ps.append(p / (jnp.sum(p, axis=-1, keepdims=True) + jnp.exp(sink - m)))
    for g in heads:
        o_ref[:, g * ATT_HEAD_DIM:(g + 1) * ATT_HEAD_DIM] = _bdot(ps[g], v3).astype(o_ref.dtype)


def banded_attention(z, sinks, q_w, kv_w):
    b, l, _ = z.shape
    blk = WINDOW
    nb = l // blk
    hd = ATT_HEAD_DIM
    kvh = kv_w // hd
    gw = ATT_GROUP * hd
    half = hd // 2
    inv = ROPE_THETA ** (-jnp.arange(half, dtype=F32) * 2.0 / hd)
    ang = jnp.arange(l, dtype=F32)[:, None] * inv[None, :]
    cos2 = jnp.concatenate([jnp.cos(ang), jnp.cos(ang)], axis=-1)
    sin2 = jnp.concatenate([-jnp.sin(ang), jnp.sin(ang)], axis=-1)
    kcol, vcol = q_w // hd, (q_w + kv_w) // hd

    def prev(i):
        return jnp.maximum(i - 1, 0)

    def nxt(i):
        return jnp.minimum(i + 1, nb - 1)

    def kv_spec(col0, which):
        return pl.BlockSpec((None, blk, hd), lambda bi, i, h: (bi, which(i), col0 + h))

    def tab_spec(which):
        return pl.BlockSpec((blk, hd), lambda bi, i, h: (which(i), 0))

    same = lambda i: i
    return pl.pallas_call(
        functools.partial(_attn_kernel, seq_len=l),
        out_shape=jax.ShapeDtypeStruct((b, l, q_w), BF16),
        grid=(b, nb, kvh),
        in_specs=[pl.BlockSpec(memory_space=pltpu.SMEM),
                  pl.BlockSpec((None, blk, gw), lambda bi, i, h: (bi, i, h)),
                  kv_spec(kcol, prev), kv_spec(kcol, same), kv_spec(kcol, nxt),
                  kv_spec(vcol, prev), kv_spec(vcol, same), kv_spec(vcol, nxt),
                  tab_spec(same), tab_spec(same), tab_spec(prev), tab_spec(prev), tab_spec(nxt), tab_spec(nxt)],
        out_specs=pl.BlockSpec((None, blk, gw), lambda bi, i, h: (bi, i, h)),
        compiler_params=_cparams("parallel", "parallel", "parallel"),
        name="banded_attention",
    )(sinks, z, z, z, z, z, z, z, cos2, sin2, cos2, sin2, cos2, sin2)


def _shift_rows(x, prev_row, next_row):
    tl = x.shape[0]
    row = lax.broadcasted_iota(jnp.int32, x.shape, 0)
    up = jnp.where(row == 0, prev_row, pltpu.roll(x, 1, axis=0))
    dn = jnp.where(row == tl - 1, next_row, pltpu.roll(x, tl - 1, axis=0))
    return up, dn


def _halo_rows(prev_ref, next_ref):
    i = pl.program_id(1)
    last = pl.num_programs(1) - 1
    prev_row = jnp.where(i == 0, 0.0, prev_ref[SUBLANES - 1:SUBLANES, :])
    next_row = jnp.where(i == last, 0.0, next_ref[0:1, :])
    return prev_row, next_row


def _hy_pro_kernel(*refs):
    groups = [refs[3 * g:3 * g + 3] for g in range(3)]
    cw_refs = refs[9:12]
    cb_refs = refs[12:15]
    bias_ref = refs[15]
    w_ref, x0_ref, t2_ref = refs[16:19]
    u = []
    for (c_ref, p_ref, n_ref), cw_ref, cb_ref in zip(groups, cw_refs, cb_refs):
        x = c_ref[...]
        prev_row, next_row = _halo_rows(p_ref, n_ref)
        up, dn = _shift_rows(x, prev_row, next_row)
        u.append(up * cw_ref[0:1, :] + x * cw_ref[1:2, :] + dn * cw_ref[2:3, :] + cb_ref[...])
    x0, x1, hv = u
    w = hv * x1
    w_ref[...] = w.astype(w_ref.dtype)
    x0_ref[...] = x0.astype(x0_ref.dtype)
    t2_ref[...] = (x0 * (w * bias_ref[...])).astype(t2_ref.dtype)


def hyena_prologue(z, conv_w, conv_b, hy_bias, col0, c):
    b, l, _ = z.shape
    tl = _tile(l, 256, SUBLANES)
    tc = _tile(c, 512, LANES)
    nh = l // SUBLANES
    tps = tl // SUBLANES
    in_specs, args = [], []
    for g in range(3):
        cb0 = (col0 + g * c) // tc
        in_specs += [
            pl.BlockSpec((None, tl, tc), lambda bi, i, j, cb0=cb0: (bi, i, cb0 + j)),
            pl.BlockSpec((None, SUBLANES, tc), lambda bi, i, j, cb0=cb0: (bi, jnp.maximum(i * tps - 1, 0), cb0 + j)),
            pl.BlockSpec((None, SUBLANES, tc), lambda bi, i, j, cb0=cb0: (bi, jnp.minimum((i + 1) * tps, nh - 1), cb0 + j)),
        ]
        args += [z, z, z]
    for g in range(3):
        in_specs.append(pl.BlockSpec((3, tc), lambda bi, i, j, g=g: (0, g * (c // tc) + j)))
        args.append(conv_w)
    for g in range(3):
        in_specs.append(pl.BlockSpec((1, tc), lambda bi, i, j, g=g: (0, g * (c // tc) + j)))
        args.append(conv_b)
    in_specs.append(pl.BlockSpec((1, tc), lambda bi, i, j: (0, j)))
    args.append(hy_bias)
    out_spec = pl.BlockSpec((None, tl, tc), lambda bi, i, j: (bi, i, j))
    return pl.pallas_call(
        _hy_pro_kernel,
        out_shape=[jax.ShapeDtypeStruct((b, l, c), F32)] * 3,
        grid=(b, l // tl, c // tc),
        in_specs=in_specs,
        out_specs=[out_spec] * 3,
        compiler_params=_cparams("parallel", "parallel", "parallel"),
        name="hyena_prologue",
    )(*args)


def _hy_filter_kernel(z_ref, w1_ref, b1_ref, f1_ref, w2_ref, b2_ref, f2_ref, w3_ref, dl_ref,
                      h_ref, asum_ref, row0_ref, *, seq_len):
    i = pl.program_id(0)
    tl = z_ref.shape[0]
    c = dl_ref.shape[1]
    h1 = jnp.sin(f1_ref[...] * (_hdot(z_ref[...], w1_ref[...]) + b1_ref[...]))
    h2 = jnp.sin(f2_ref[...] * (_hdot(h1, w2_ref[...]) + b2_ref[...]))
    h3 = _hdot(h2, w3_ref[...])
    row = i * tl + lax.broadcasted_iota(jnp.int32, (tl, 1), 0)
    t = row.astype(F32) * (1.0 / (seq_len - 1))
    window = jnp.exp(-t * dl_ref[...]) + HY_MOD_SHIFT
    h3 = h3 * jnp.concatenate([window, window], axis=1)
    h_ref[...] = h3

    @pl.when(i == 0)
    def _():
        asum_ref[...] = jnp.zeros_like(asum_ref)
        row0_ref[...] = h3[0:1, :]

    asum_ref[...] += jnp.sum(jnp.where(row == 0, 0.0, jnp.abs(h3)), axis=0, keepdims=True)


def hyena_filter(l, w1, b1, f1, w2, b2, f2, w3, c):
    t = jnp.linspace(0.0, 1.0, l, dtype=F32)[:, None]
    w = 2.0 * math.pi * jnp.arange(l, dtype=F32) / l
    bands = jnp.linspace(1e-4, HY_BANDS - 1, HY_BANDS, dtype=F32)
    ang = w[:, None] * bands[None, :]
    z = jnp.concatenate([t, jnp.cos(ang), -jnp.sin(ang)], axis=-1)
    emb = z.shape[1]
    emb_pad = -(-emb // SUBLANES) * SUBLANES
    z = jnp.pad(z, ((0, 0), (0, emb_pad - emb)))
    w1 = jnp.pad(w1, ((0, emb_pad - emb), (0, 0)))
    max_decay = math.log(HY_DECAY_TARGET) / HY_FAST_DECAY
    min_decay = math.log(HY_DECAY_TARGET) / HY_SLOW_DECAY
    deltas = jnp.abs(jnp.linspace(min_decay, max_decay, c, dtype=F32))[None, :]
    ffn = w2.shape[0]
    tl = _tile(l, 512, SUBLANES)
    full = lambda shape: pl.BlockSpec(shape, lambda i: (0, 0))
    h, asum, row0 = pl.pallas_call(
        functools.partial(_hy_filter_kernel, seq_len=l),
        out_shape=[jax.ShapeDtypeStruct((l, 2 * c), F32), jax.ShapeDtypeStruct((1, 2 * c), F32),
                   jax.ShapeDtypeStruct((1, 2 * c), F32)],
        grid=(l // tl,),
        in_specs=[pl.BlockSpec((tl, emb_pad), lambda i: (i, 0)), full((emb_pad, ffn)), full((1, ffn)), full((1, ffn)),
                  full((ffn, ffn)), full((1, ffn)), full((1, ffn)), full((ffn, 2 * c)), full((1, c))],
        out_specs=[pl.BlockSpec((tl, 2 * c), lambda i: (i, 0)), full((1, 2 * c)), full((1, 2 * c))],
        compiler_params=_cparams("arbitrary"),
        name="hyena_filter",
    )(z, w1, b1, f1, w2, b2, f2, w3, deltas)
    norm = asum[:, :c] + asum[:, c:] + jnp.abs(row0[:, :c] + row0[:, c:])
    return h, norm


def _dft_tables(n1, n2):
    n = n1 * n2
    n1h = n1 // 2
    th1 = 2.0 * np.pi * np.outer(np.arange(n1), np.arange(n1h)) / n1
    f1 = np.concatenate([np.cos(th1), -np.sin(th1)], axis=0)
    th1i = 2.0 * np.pi * np.outer(np.arange(n1h), np.arange(n1)) / n1
    f1inv_re, f1inv_im = np.cos(th1i), -np.sin(th1i)
    k1 = np.arange(n1)[:, None, None]
    k2 = np.arange(n2)[None, :, None]
    m2 = np.arange(n2)[None, None, :]
    th = 2.0 * np.pi * (m2 * k2 / n2 + m2 * k1 / n)
    gr, gi = np.cos(th), -np.sin(th)
    g = np.concatenate([np.concatenate([gr, -gi], axis=2), np.concatenate([gi, gr], axis=2)], axis=1)
    grt, git = np.swapaxes(gr, 1, 2), np.swapaxes(gi, 1, 2)
    ginv = np.concatenate([np.concatenate([grt, git], axis=2), np.concatenate([-git, grt], axis=2)], axis=1)
    to = lambda a: jnp.asarray(a.astype(np.float32)).astype(BF16)
    return to(f1), (to(f1inv_re), to(f1inv_im)), to(g), to(ginv)


def _dft_a_kernel(w_ref, x_ref, o_ref):
    xt = pltpu.einshape("msc->smc", x_ref[...])
    out = [_bdot(w_ref[...], xt[s]) for s in range(x_ref.shape[1])]
    o_ref[...] = pltpu.einshape("smc->msc", jnp.stack(out, axis=0))


def dft_stage_a(w, x):
    b, k, n2, c = x.shape
    m = w.shape[0]
    tc = _tile(c, max(LANES, DFT_BLOCK_BYTES // (m * SUBLANES * 4)), LANES)
    return pl.pallas_call(
        _dft_a_kernel,
        out_shape=jax.ShapeDtypeStruct((b, m, n2, c), F32),
        grid=(b, n2 // SUBLANES, c // tc),
        in_specs=[pl.BlockSpec((m, k), lambda bi, i, j: (0, 0)),
                  pl.BlockSpec((None, k, SUBLANES, tc), lambda bi, i, j: (bi, 0, i, j))],
        out_specs=pl.BlockSpec((None, m, SUBLANES, tc), lambda bi, i, j: (bi, 0, i, j)),
        compiler_params=_cparams("parallel", "parallel", "parallel"),
        name="dft_stage_a",
    )(w, x)


def _dft_mid_fwd_kernel(g_ref, fr_ref, fi_ref, br_ref, bi_ref, hr_ref, hi_ref):
    n2 = fr_ref.shape[0]
    xf = _bdot(g_ref[...], jnp.concatenate([fr_ref[...], fi_ref[...]], axis=0))
    xb = _bdot(g_ref[...], jnp.concatenate([br_ref[...], bi_ref[...]], axis=0))
    hr_ref[...] = xf[:n2] + xb[:n2]
    hi_ref[...] = xf[n2:] - xb[n2:]


def dft_mid_forward(g, a):
    _, _, n1, n2, c2 = a.shape
    c = c2 // 2
    tc = _tile(c, 2048, LANES)
    nct = c // tc
    a_spec = lambda ri, off: pl.BlockSpec((None, None, None, n2, tc), lambda k, j: (0, ri, k, 0, off + j))
    o_spec = pl.BlockSpec((None, n2, tc), lambda k, j: (k, 0, j))
    spec = jax.ShapeDtypeStruct((n1, n2, c), F32)
    return pl.pallas_call(
        _dft_mid_fwd_kernel,
        out_shape=[spec, spec],
        grid=(n1, nct),
        in_specs=[pl.BlockSpec((None, 2 * n2, 2 * n2), lambda k, j: (k, 0, 0)),
                  a_spec(0, 0), a_spec(1, 0), a_spec(0, nct), a_spec(1, nct)],
        out_specs=[o_spec, o_spec],
        compiler_params=_cparams("parallel", "parallel"),
        name="dft_mid_forward",
    )(g, a, a, a, a)


def _dft_mid_kernel(g_ref, gi_ref, ar_ref, ai_ref, hr_ref, hi_ref, br_ref, bi_ref):
    n2 = ar_ref.shape[0]
    x = _bdot(g_ref[...], jnp.concatenate([ar_ref[...], ai_ref[...]], axis=0))
    xr, xi = x[:n2], x[n2:]
    hr, hi = hr_ref[...], hi_ref[...]
    y = jnp.concatenate([xr * hr - xi * hi, xr * hi + xi * hr], axis=0)
    bm = _bdot(gi_ref[...], y)
    br_ref[...] = bm[:n2].astype(br_ref.dtype)
    bi_ref[...] = bm[n2:].astype(bi_ref.dtype)


def dft_mid(g, ginv, a, hr, hi):
    b, _, n1, n2, c = a.shape
    tc = _tile(c, 2048, LANES)
    a_spec = lambda ri: pl.BlockSpec((None, None, None, n2, tc), lambda k, bi, j: (bi, ri, k, 0, j))
    h_spec = pl.BlockSpec((None, n2, tc), lambda k, bi, j: (k, 0, j))
    g_spec = pl.BlockSpec((None, 2 * n2, 2 * n2), lambda k, bi, j: (k, 0, 0))
    o_spec = pl.BlockSpec((None, None, n2, tc), lambda k, bi, j: (bi, k, 0, j))
    out = jax.ShapeDtypeStruct((b, n1, n2, c), F32)
    return pl.pallas_call(
        _dft_mid_kernel,
        out_shape=[out, out],
        grid=(n1, b, c // tc),
        in_specs=[g_spec, g_spec, a_spec(0), a_spec(1), h_spec, h_spec],
        out_specs=[o_spec, o_spec],
        compiler_params=_cparams("parallel", "parallel", "parallel"),
        name="dft_mid",
    )(g, ginv, a, a, hr, hi)


def _dft_out_kernel(fr_ref, fi_ref, br_ref, bi_ref, x0_ref, t2_ref, sc_ref, o_ref):
    brt = pltpu.einshape("msc->smc", br_ref[...])
    bit = pltpu.einshape("msc->smc", bi_ref[...])
    y = jnp.stack([_bdot(fr_ref[...], brt[s]) + _bdot(fi_ref[...], bit[s]) for s in range(o_ref.shape[1])], axis=0)
    o_ref[...] = x0_ref[...] * (pltpu.einshape("smc->msc", y) * sc_ref[...]) + t2_ref[...]


def dft_out(f1inv, br, bi, x0, t2, scale):
    b, n1, n2, c = br.shape
    n1h = f1inv[0].shape[0]
    tc = _tile(c, max(LANES, DFT_BLOCK_BYTES // (2 * n1 * SUBLANES * 4)), LANES)
    io = pl.BlockSpec((None, n1h, SUBLANES, tc), lambda bi_, i, j: (bi_, 0, i, j))
    bspec = pl.BlockSpec((None, n1, SUBLANES, tc), lambda bi_, i, j: (bi_, 0, i, j))
    fspec = pl.BlockSpec((n1h, n1), lambda bi_, i, j: (0, 0))
    return pl.pallas_call(
        _dft_out_kernel,
        out_shape=jax.ShapeDtypeStruct((b, n1h, n2, c), F32),
        grid=(b, n2 // SUBLANES, c // tc),
        in_specs=[fspec, fspec, bspec, bspec, io, io, pl.BlockSpec((1, tc), lambda bi_, i, j: (0, j))],
        out_specs=io,
        compiler_params=_cparams("parallel", "parallel", "parallel"),
        name="dft_stage_a_inverse",
    )(f1inv[0], f1inv[1], br, bi, x0, t2, scale)


def hyena_long_conv(w, x0, t2, hfilt, norm):
    b, l, c = w.shape
    n2 = DFT_N2
    n1 = 2 * l // n2
    n1h = n1 // 2
    f1, f1inv, g, ginv = _dft_tables(n1, n2)
    ha = dft_stage_a(f1, hfilt.reshape(1, n1h, n2, 2 * c))
    hr, hi = dft_mid_forward(g, ha.reshape(1, 2, n1, n2, 2 * c))
    a = dft_stage_a(f1, w.reshape(b, n1h, n2, c))
    br, bi = dft_mid(g, ginv, a.reshape(b, 2, n1, n2, c), hr, hi)
    scale = 1.0 / (norm * (2.0 * l))
    y = dft_out(f1inv, br, bi, x0.reshape(b, n1h, n2, c), t2.reshape(b, n1h, n2, c), scale)
    return y.reshape(b, l, c)


def even_mixer(h, z_w, p, i, q_w):
    b, l, d = h.shape
    c = d - q_w
    in_w = z_w.shape[1]
    kv_w = (in_w - q_w - 3 * c) // 2
    z = matmul(h.reshape(b * l, d), z_w, name="even_in_proj").reshape(b, l, in_w)
    att = banded_attention(z, p["ev_sinks"][i], q_w, kv_w)
    w16, x0, t2 = hyena_prologue(z, p["ev_conv_w"][i], p["ev_conv_b"][i][None], p["ev_hy_bias"][i][None],
                                 q_w + 2 * kv_w, c)
    hfilt, norm = hyena_filter(l, p["ev_filt_w1"][i], p["ev_filt_b1"][i][None], p["ev_filt_f1"][i][None],
                               p["ev_filt_w2"][i], p["ev_filt_b2"][i][None], p["ev_filt_f2"][i][None],
                               p["ev_filt_w3"][i], c)
    y_hy = hyena_long_conv(w16, x0, t2, hfilt, norm)
    return att.reshape(b * l, q_w), y_hy.reshape(b * l, c)


def _rw_mix_kernel(h_ref, p_ref, n_ref, lerp_ref, *o_refs):
    h = h_ref[...]
    prev_row, next_row = _halo_rows(p_ref, n_ref)
    up, dn = _shift_rows(h, prev_row, next_row)
    xx = 0.5 * (up + dn) - h
    for n, o_ref in enumerate(o_refs):
        o_ref[...] = (h + xx * lerp_ref[n:n + 1, :]).astype(o_ref.dtype)


def rwkv_mix(h, lerp):
    b, l, d = h.shape
    n = lerp.shape[0]
    tl = _tile(l, 256, SUBLANES)
    tc = _tile(d, 512, LANES)
    nh = l // SUBLANES
    tps = tl // SUBLANES
    blk = pl.BlockSpec((None, tl, tc), lambda bi, i, j: (bi, i, j))
    return pl.pallas_call(
        _rw_mix_kernel,
        out_shape=[jax.ShapeDtypeStruct((b, l, d), BF16)] * n,
        grid=(b, l // tl, d // tc),
        in_specs=[blk,
                  pl.BlockSpec((None, SUBLANES, tc), lambda bi, i, j: (bi, jnp.maximum(i * tps - 1, 0), j)),
                  pl.BlockSpec((None, SUBLANES, tc), lambda bi, i, j: (bi, jnp.minimum((i + 1) * tps, nh - 1), j)),
                  pl.BlockSpec((n, tc), lambda bi, i, j: (0, j))],
        out_specs=[blk] * n,
        compiler_params=_cparams("parallel", "parallel", "parallel"),
        name="rwkv_mix",
    )(h, h, h, lerp)


def _head_sum(x, m0):
    s0 = jnp.sum(jnp.where(m0, x, 0.0), axis=-1, keepdims=True)
    s1 = jnp.sum(jnp.where(m0, 0.0, x), axis=-1, keepdims=True)
    return jnp.where(m0, s0, s1)


def _wkv_kernel(r_ref, k_ref, v_ref, lw_ref, a_ref, kk_ref, ka_ref, rk_ref, y_ref, bon_ref, s_ref, *, rev):
    @pl.when(pl.program_id(2) == 0)
    def _():
        s_ref[...] = jnp.zeros_like(s_ref)

    pair = 2 * RW_HEAD
    pairs = range(s_ref.shape[0])
    c = r_ref.shape[0]
    c2 = 2 * c
    prep = [_wkv_prep(*(ref[:, g * pair:(g + 1) * pair] for ref in
                        (r_ref, k_ref, v_ref, lw_ref, a_ref, kk_ref, ka_ref, rk_ref)), rev) for g in pairs]
    for g in pairs:
        bon_ref[:, g * pair:(g + 1) * pair] = prep[g]["bonus"]
    state = [s_ref[g] for g in pairs]

    row2 = lax.broadcasted_iota(jnp.int32, (c2, c2), 0)
    col2 = lax.broadcasted_iota(jnp.int32, (c2, c2), 1)
    strict = (col2 > row2) if rev else (col2 < row2)
    incl = (col2 >= row2) if rev else (col2 <= row2)
    m_all = [_bdot_nt(jnp.concatenate([p["aq"], p["rq"]], axis=0), jnp.concatenate([p["bd"], p["kd"]], axis=0))
             for p in prep]
    m_ab = [jnp.where(strict, m[:c2, :c2], 0.0) for m in m_all]
    m_ak = [jnp.where(strict, m[:c2, c2:], 0.0) for m in m_all]
    m_rb = [jnp.where(incl, m[c2:, :c2], 0.0) for m in m_all]
    m_rk = [jnp.where(incl, m[c2:, c2:], 0.0) for m in m_all]

    eye = jnp.where(row2 == col2, 1.0, 0.0)
    x = [eye + m for m in m_ab]
    pw = m_ab
    for _ in range(int(math.log2(c)) - 1):
        pw = [_bdot(p, p) for p in pw]
        x = [xi + _bdot(xi, p) for xi, p in zip(x, pw)]

    ar = [_bdot_nt(jnp.concatenate([p["aq"], p["rq"]], axis=0), s) for p, s in zip(prep, state)]
    mv = [_bdot(jnp.concatenate([mk, mr], axis=0), p["v"]) for p, mk, mr in zip(prep, m_ak, m_rk)]
    part = [a + b for a, b in zip(ar, mv)]
    us = [_bdot(xi, t[:c2]) for xi, t in zip(x, part)]
    ys = [t[c2:] + _bdot(m, u) for t, m, u in zip(part, m_rb, us)]
    upd = [lax.dot_general(jnp.concatenate([p["v"], u], axis=0).astype(BF16),
                           jnp.concatenate([p["kc"], p["bc"]], axis=0).astype(BF16),
                           (((0,), (0,)), ((), ())), preferred_element_type=F32) for p, u in zip(prep, us)]
    for g in pairs:
        y_ref[:, g * pair:(g + 1) * pair] = ys[g][:c] + ys[g][c:]
        s_ref[g] = state[g] * prep[g]["decay"] + upd[g]


def _wkv_prep(r, k, v, lw, a, k_k, k_a, r_k, rev):
    c = r.shape[0]
    m0 = lax.broadcasted_iota(jnp.int32, r.shape, 1) < RW_HEAD
    kk = k * k_k
    kk = kk / jnp.maximum(jnp.sqrt(_head_sum(kk * kk, m0)), 1e-12)
    kd = k * (1.0 + (a - 1.0) * k_a)
    bvec = kk * a

    row = lax.broadcasted_iota(jnp.int32, (c, c), 0)
    col = lax.broadcasted_iota(jnp.int32, (c, c), 1)
    tri = jnp.where((col >= row) if rev else (col <= row), 1.0, 0.0)
    hi = lw.astype(BF16)
    rest = lw - hi.astype(F32)
    mid = rest.astype(BF16)
    ci = _bdot(tri, hi) + _bdot(tri, mid) + _bdot(tri, rest - mid.astype(F32))
    ctot = jnp.sum(lw, axis=0, keepdims=True)
    inv = jnp.exp(-ci)
    tail = jnp.exp(ctot - ci)

    def stack(x):
        return jnp.concatenate([jnp.where(m0, x, 0.0), jnp.where(m0, 0.0, x)], axis=0)

    return dict(aq=stack(-kk * jnp.exp(ci - lw)), rq=stack(r * jnp.exp(ci)), kd=stack(kd * inv), bd=stack(bvec * inv),
                kc=stack(kd * tail), bc=stack(bvec * tail), v=stack(v), decay=jnp.exp(ctot),
                bonus=_head_sum(r * kd * r_k, m0) * v)


def wkv_scan(r, k, v, lw, a, k_k, k_a, r_k, rev):
    b, l, d = r.shape
    c = RW_CHUNK
    nch = l // c
    pair = 2 * RW_HEAD
    npairs = d // pair
    group = next(g for g in (WKV_PAIRS_PER_STEP, 3, 2, 1) if npairs % g == 0)
    wide = group * pair
    cidx = (lambda ci: nch - 1 - ci) if rev else (lambda ci: ci)
    seq = pl.BlockSpec((None, c, wide), lambda bi, hp, ci: (bi, cidx(ci), hp))
    par = pl.BlockSpec((1, wide), lambda bi, hp, ci: (0, hp))
    out = jax.ShapeDtypeStruct((b, l, d), F32)
    return pl.pallas_call(
        functools.partial(_wkv_kernel, rev=rev),
        out_shape=[out, out],
        grid=(b, npairs // group, nch),
        in_specs=[seq] * 5 + [par] * 3,
        out_specs=[seq, seq],
        scratch_shapes=[pltpu.VMEM((group, pair, pair), F32)],
        compiler_params=_cparams("parallel", "parallel", "arbitrary"),
        name="wkv_scan_bwd" if rev else "wkv_scan_fwd",
    )(r, k, v, lw, a, k_k, k_a, r_k)


def _rw_post_kernel(yf_ref, yb_ref, bf_ref, bb_ref, g_ref, gg_ref, gb_ref, o_ref):
    pair = 2 * RW_HEAD
    m0 = lax.broadcasted_iota(jnp.int32, (yf_ref.shape[0], pair), 1) < RW_HEAD
    for c0 in range(0, yf_ref.shape[1], pair):
        sl = slice(c0, c0 + pair)
        y = yf_ref[:, sl] + yb_ref[:, sl]
        mu = _head_sum(y, m0) * (1.0 / RW_HEAD)
        dlt = y - mu
        var = _head_sum(dlt * dlt, m0) * (1.0 / RW_HEAD)
        yn = dlt * lax.rsqrt(var + RW_GN_EPS) * gg_ref[:, sl] + gb_ref[:, sl] + bf_ref[:, sl] + bb_ref[:, sl]
        o_ref[:, sl] = (yn * g_ref[:, sl]).astype(o_ref.dtype)


def rwkv_post(yf, yb, bf, bb, g, gn_g, gn_b):
    b, l, d = yf.shape
    pair = _tile(d, 1024, 2 * RW_HEAD)
    tl = _tile(l, 256, SUBLANES)
    blk = pl.BlockSpec((None, tl, pair), lambda bi, i, j: (bi, i, j))
    par = pl.BlockSpec((1, pair), lambda bi, i, j: (0, j))
    return pl.pallas_call(
        _rw_post_kernel,
        out_shape=jax.ShapeDtypeStruct((b, l, d), BF16),
        grid=(b, l // tl, d // pair),
        in_specs=[blk] * 5 + [par, par],
        out_specs=blk,
        compiler_params=_cparams("parallel", "parallel", "parallel"),
        name="rwkv_post",
    )(yf, yb, bf, bb, g, gn_g, gn_b)


def rwkv_mixer(h, p, j):
    b, l, d = h.shape
    t = b * l
    xr, xw, xk, xv, xa, xg = (x.reshape(t, d) for x in rwkv_mix(h, p["od_lerp"][j]))
    r = matmul(xr, p["od_w_r"][j], name="rwkv_r")
    k = matmul(xk, p["od_w_k"][j], name="rwkv_k")
    v = matmul(xv, p["od_w_v"][j], name="rwkv_v")
    g = matmul(matmul(xg, p["od_g1"][j], name="rwkv_g1"), p["od_g2"][j], in_act="sigmoid", name="rwkv_g2")
    w1 = jnp.concatenate([p["od_w1"][j][0], p["od_w1"][j][1]], axis=1)
    a1 = jnp.concatenate([p["od_a1"][j][0], p["od_a1"][j][1]], axis=1)
    tw = matmul(xw, w1, name="rwkv_w1")
    ta = matmul(xa, a1, name="rwkv_a1")
    nl = tw.shape[1] // 2
    na = ta.shape[1] // 2
    shp = (b, l, d)
    ys, bons = [], []
    for di, rev in ((0, False), (1, True)):
        lw = matmul(tw[:, di * nl:(di + 1) * nl], p["od_w2"][j][di], bias=p["od_w0"][j][di][None], in_act="tanh",
                    out_act="log_decay", name="rwkv_w2")
        a = matmul(ta[:, di * na:(di + 1) * na], p["od_a2"][j][di], bias=p["od_a0"][j][di][None],
                   out_act="sigmoid", name="rwkv_a2")
        y, bon = wkv_scan(r.reshape(shp), k.reshape(shp), v.reshape(shp), lw.reshape(shp), a.reshape(shp),
                          p["od_k_k"][j][None], p["od_k_a"][j][None], p["od_r_k"][j].reshape(1, d), rev)
        ys.append(y)
        bons.append(bon)
    out = rwkv_post(ys[0], ys[1], bons[0], bons[1], g.reshape(shp), p["od_gn_g"][j][None], p["od_gn_b"][j][None])
    return out.reshape(t, d)


def _top_values(s, k):
    rows = lax.broadcasted_iota(jnp.int32, (k, s.shape[1]), 0)

    def body(i, carry):
        s, vals = carry
        m = jnp.max(s, axis=0, keepdims=True)
        return jnp.where(s >= m, -jnp.inf, s), jnp.where(rows == i, m, vals)

    return lax.fori_loop(0, k, body, (s, jnp.full((k, s.shape[1]), -jnp.inf, F32)))[1]


def _pair_candidates(v1, v2):
    k = v1.shape[0]
    assert k == 2 * SUBLANES
    row = lax.broadcasted_iota(jnp.int32, (SUBLANES, v1.shape[1]), 0)
    groups = [v1[0:1, :] + v2, v1[1:2, :] + v2[0:SUBLANES, :]]
    for i in range(2, SUBLANES):
        groups.append(jnp.where(row < k // (i + 1), v1[i:i + 1, :] + v2[0:SUBLANES, :], -jnp.inf))
    groups.append(v1[SUBLANES:k, :] + v2[0:1, :])
    return jnp.concatenate(groups, axis=0)


def _peer_topk_kernel(q_ref, sub_ref, s_ref, e_ref, tau_ref):
    k = PEER_TOPK
    for h in range(tau_ref.shape[0]):
        s1 = _bdot(sub_ref[2 * h], q_ref[2 * h])
        s2 = _bdot(sub_ref[2 * h + 1], q_ref[2 * h + 1])
        v1 = _top_values(s1, k)
        v2 = _top_values(s2, k)
        top = _top_values(_pair_candidates(v1, v2), k)
        z = jnp.sum(jnp.exp(top - top[0:1, :]), axis=0, keepdims=True)
        s_ref[2 * h] = s1
        s_ref[2 * h + 1] = s2
        e_ref[2 * h] = jnp.exp(s1 - v1[0:1, :])
        e_ref[2 * h + 1] = jnp.exp(s2 - v2[0:1, :]) / z
        tau_ref[h:h + 1, :] = top[k - 1:k, :]


def peer_topk(q_t, sub):
    hp, dk, t = q_t.shape
    nk = sub.shape[1]
    tm = _tile(t, 512, LANES)
    blk = pl.BlockSpec((hp, nk, tm), lambda i: (0, 0, i))
    big = jax.ShapeDtypeStruct((hp, nk, t), F32)
    return pl.pallas_call(
        _peer_topk_kernel,
        out_shape=[big, big, jax.ShapeDtypeStruct((hp // 2, t), F32)],
        grid=(t // tm,),
        in_specs=[pl.BlockSpec((hp, dk, tm), lambda i: (0, 0, i)), pl.BlockSpec((hp, nk, dk), lambda i: (0, 0, 0))],
        out_specs=[blk, blk, pl.BlockSpec((hp // 2, tm), lambda i: (0, i))],
        compiler_params=_cparams("parallel"),
        name="peer_topk",
    )(q_t, sub)


def _peer_gate(s_ref, e_ref, tau_ref, a):
    gate = None
    for h in range(tau_ref.shape[0]):
        s1 = s_ref[2 * h, pl.ds(a, 1), :]
        e1 = e_ref[2 * h, pl.ds(a, 1), :]
        hit = (s1 + s_ref[2 * h + 1]) >= tau_ref[h:h + 1, :]
        w = jnp.where(hit, e1 * e_ref[2 * h + 1], 0.0)
        gate = w if gate is None else gate + w
    return gate


def _gelu(x):
    return 0.5 * x * (1.0 + lax.erf(x * (2.0 ** -0.5)))


def _peer_main_kernel(u_ref, x_ref, v_ref, s_ref, e_ref, tau_ref, o_ref, acc_ref, *, na):
    j = pl.program_id(1)
    nk = s_ref.shape[1]
    ts = na * nk

    @pl.when(j == 0)
    def _():
        acc_ref[...] = jnp.zeros_like(acc_ref)

    acts = [_bdot(u_ref[i * ts:(i + 1) * ts, :], x_ref[...]) for i in range(2)]
    for i in range(2):
        act = _gelu(acts[i])
        pieces = [(_peer_gate(s_ref, e_ref, tau_ref, (2 * j + i) * na + al) * act[al * nk:(al + 1) * nk, :]).astype(BF16)
                  for al in range(na)]
        acc_ref[...] += jnp.dot(v_ref[:, i * ts:(i + 1) * ts], jnp.concatenate(pieces, axis=0),
                                preferred_element_type=F32)

    @pl.when(j == pl.num_programs(1) - 1)
    def _():
        d = o_ref.shape[1]
        step = _tile(d, o_ref.shape[0], LANES)
        for c0 in range(0, d, step):
            o_ref[:, c0:c0 + step] = acc_ref[c0:c0 + step, :].T


def peer_main(u, x_t, v_t, s, e, tau, layer):
    _, ne, d = u.shape
    t = x_t.shape[1]
    nk = s.shape[1]
    tm = _tile(t, 512, LANES)
    na = PEER_KEYS_PER_SUBTILE
    te = 2 * na * nk
    once = pl.Buffered(1)
    tok = lambda shape: pl.BlockSpec(shape, lambda i, j: (0,) * (len(shape) - 1) + (i,), pipeline_mode=once)
    return pl.pallas_call(
        functools.partial(_peer_main_kernel, na=na),
        out_shape=jax.ShapeDtypeStruct((t, d), F32),
        grid=(t // tm, ne // te),
        in_specs=[pl.BlockSpec((None, te, d), lambda i, j: (layer, j, 0)), tok((d, tm)),
                  pl.BlockSpec((None, d, te), lambda i, j: (layer, 0, j)),
                  tok(s.shape[:2] + (tm,)), tok(e.shape[:2] + (tm,)), tok((tau.shape[0], tm))],
        out_specs=pl.BlockSpec((tm, d), lambda i, j: (i, 0), pipeline_mode=once),
        scratch_shapes=[pltpu.VMEM((d, tm), F32)],
        compiler_params=_cparams("parallel", "arbitrary"),
        name="peer_main",
    )(u, x_t, v_t, s, e, tau)


def peer(x_t, wq_t, sub, u, v_t, layer):
    q_t = matmul(wq_t, x_t, layer=layer, name="peer_query")
    hp, nk, dk = sub.shape[1:]
    s, e, tau = peer_topk(q_t.reshape(hp, dk, -1), sub[layer])
    return peer_main(u, x_t, v_t, s, e, tau, layer)


def _prepare(w):
    p = dict(w)
    for name in ("ev_w_in", "ev_w_out", "od_w_r", "od_w_k", "od_w_v", "od_w_o", "od_w1", "od_w2", "od_a1", "od_a2",
                 "od_g1", "od_g2", "pk_u"):
        p[name] = w[name].astype(BF16)
    p["pk_v_t"] = jnp.swapaxes(w["pk_v"].astype(BF16), 1, 2)
    p["pk_wq_t"] = jnp.swapaxes(w["pk_w_q"].astype(BF16), 1, 2)
    sk = w["pk_sub_keys"]
    p["pk_sub"] = sk.astype(BF16).reshape(sk.shape[0], sk.shape[1] * sk.shape[2], sk.shape[3], sk.shape[4])
    return p


def _trunk(x, mods, p):
    b, l, d = x.shape
    t = b * l
    depth = p["ada_w"].shape[0]
    q_w = p["ev_sinks"].shape[1] * ATT_HEAD_DIM
    y = gt2 = None
    for layer in range(depth):
        sh1, sc1, gt1, sh2, sc2, gt2_next = mods[layer]
        even = layer % 2 == 0
        g1 = p["norm_g"][layer, 0][None]
        kw = dict(shift=sh1, scale=sc1, out_dtype=BF16 if even else F32)
        if y is None:
            h = norm_mod(x, g1, **kw)
        else:
            x, h = norm_mod(x, g1, res=y, gate=gt2, emit_x=True, **kw)
        kw = dict(res=x.reshape(t, d), gate=gt1, rows_per_gate=l, name="mixer_out_proj")
        if even:
            att, hy = even_mixer(h, p["ev_w_in"][layer // 2], p, layer // 2, q_w)
            w_out = p["ev_w_out"][layer // 2]
            x = matmul(att, w_out[:q_w], x2=hy, w2=w_out[q_w:], **kw)
        else:
            x = matmul(rwkv_mixer(h, p, layer // 2), p["od_w_o"][layer // 2], **kw)
        x = x.reshape(b, l, d)
        h2_t = norm_mod(x, p["norm_g"][layer, 1][None], shift=sh2, scale=sc2, transposed=True)
        y = peer(h2_t, p["pk_wq_t"], p["pk_sub"], p["pk_u"], p["pk_v_t"], layer)
        y = y.reshape(b, l, d)
        gt2 = gt2_next
    return norm_mod(x, p["final_g"][None], res=y, gate=gt2, out_dtype=F32)


def kernel(x_prompt, x_sample, c_prompt, c_sample, ada_w, ada_b, norm_g, final_g, ev_w_in, ev_sinks, ev_conv_w, ev_conv_b, ev_filt_w1, ev_filt_b1, ev_filt_f1, ev_filt_w2, ev_filt_b2, ev_filt_f2, ev_filt_w3, ev_hy_bias, ev_w_out, od_lerp, od_w_r, od_w_k, od_w_v, od_w_o, od_w0, od_w1, od_w2, od_a0, od_a1, od_a2, od_g1, od_g2, od_k_k, od_k_a, od_r_k, od_gn_g, od_gn_b, pk_w_q, pk_sub_keys, pk_u, pk_v):
    p = _prepare(dict(
        ada_w=ada_w, ada_b=ada_b, norm_g=norm_g, final_g=final_g, ev_w_in=ev_w_in, ev_sinks=ev_sinks,
        ev_conv_w=ev_conv_w, ev_conv_b=ev_conv_b, ev_filt_w1=ev_filt_w1, ev_filt_b1=ev_filt_b1, ev_filt_f1=ev_filt_f1,
        ev_filt_w2=ev_filt_w2, ev_filt_b2=ev_filt_b2, ev_filt_f2=ev_filt_f2, ev_filt_w3=ev_filt_w3,
        ev_hy_bias=ev_hy_bias, ev_w_out=ev_w_out, od_lerp=od_lerp, od_w_r=od_w_r, od_w_k=od_w_k, od_w_v=od_w_v,
        od_w_o=od_w_o, od_w0=od_w0, od_w1=od_w1, od_w2=od_w2, od_a0=od_a0, od_a1=od_a1, od_a2=od_a2, od_g1=od_g1,
        od_g2=od_g2, od_k_k=od_k_k, od_k_a=od_k_a, od_r_k=od_r_k, od_gn_g=od_gn_g, od_gn_b=od_gn_b,
        pk_w_q=pk_w_q, pk_sub_keys=pk_sub_keys, pk_u=pk_u, pk_v=pk_v))
    depth, d = ada_w.shape[0], ada_w.shape[1]
    bp, bs = c_prompt.shape[0], c_sample.shape[0]
    rows = -(-(bp + bs) // SUBLANES) * SUBLANES
    c_all = jnp.pad(jnp.concatenate([c_prompt, c_sample], axis=0), ((0, rows - bp - bs), (0, 0)))
    mods_p, mods_s = [], []
    for layer in range(depth):
        mod = matmul(c_all, ada_w, bias=ada_b[layer][None], in_act="silu", layer=layer, name="adaln")
        parts = jnp.split(mod, 6, axis=-1)
        mods_p.append([m[:bp, None, :] for m in parts])
        mods_s.append([m[bp:bp + bs, None, :] for m in parts])
    return _trunk(x_prompt, mods_p, p), _trunk(x_sample, mods_s, p)
```

```python
import functools
import math

import jax
import jax.numpy as jnp
import numpy as np
from jax import lax
from jax.experimental import pallas as pl
from jax.experimental.pallas import tpu as pltpu

F32 = jnp.float32
BF16 = jnp.bfloat16

LANES = 128
SUBLANES = 8
VMEM_LIMIT_BYTES = 56 * 1024 * 1024

NORM_EPS = 1e-6
ATT_HEAD_DIM = 128
ATT_GROUP = 4
WINDOW = 128
ROPE_THETA = 10000.0
HY_BANDS = 16
HY_DECAY_TARGET = 1e-2
HY_FAST_DECAY = 0.3
HY_SLOW_DECAY = 1.5
HY_MOD_SHIFT = 0.05
RW_HEAD = 64
RW_GN_EPS = 64e-5
RW_CHUNK = 64
WKV_PAIRS_PER_STEP = 16
PEER_NKEYS = 128
PEER_TOPK = 16
PEER_KEYS_PER_SUBTILE = 2
DFT_N2 = 128
DFT_BLOCK_BYTES = 8 * 1024 * 1024


def _cparams(*sem):
    return pltpu.CompilerParams(dimension_semantics=sem, vmem_limit_bytes=VMEM_LIMIT_BYTES)


def _tile(n, target, mult):
    t = min(n, target)
    t -= t % mult
    while t >= mult:
        if n % t == 0:
            return t
        t -= mult
    return n


def _bdot(a, b):
    return jnp.dot(a.astype(BF16), b.astype(BF16), preferred_element_type=F32)


def _bdot_nt(a, b):
    return lax.dot_general(a.astype(BF16), b.astype(BF16), (((1,), (1,)), ((), ())), preferred_element_type=F32)


def _hdot(a, b):
    return jnp.dot(a.astype(F32), b.astype(F32), preferred_element_type=F32, precision=lax.Precision.HIGHEST)


def _silu(x):
    return x * jax.nn.sigmoid(x)


def _log_decay(z):
    return -jnp.exp(-jax.nn.softplus(-z) - 0.5)


_ACTS = {None: None, "silu": _silu, "tanh": jnp.tanh, "sigmoid": jax.nn.sigmoid, "log_decay": _log_decay}


def _mm_kernel(*refs, in_act, out_act, has_bias, has_res, has_x2):
    x_ref, w_ref = refs[0], refs[1]
    pos = 2
    bias_ref = res_ref = gate_ref = x2_ref = w2_ref = None
    if has_x2:
        x2_ref, w2_ref = refs[pos], refs[pos + 1]
        pos += 2
    if has_bias:
        bias_ref = refs[pos]
        pos += 1
    if has_res:
        res_ref, gate_ref = refs[pos], refs[pos + 1]
        pos += 2
    o_ref = refs[pos]
    x = x_ref[...]
    if in_act is not None:
        x = _ACTS[in_act](x.astype(F32))
    acc = _bdot(x, w_ref[...])
    if has_x2:
        acc = acc + _bdot(x2_ref[...], w2_ref[...])
    if has_bias:
        acc = acc + bias_ref[...]
    if out_act is not None:
        acc = _ACTS[out_act](acc)
    if has_res:
        acc = res_ref[...] + gate_ref[...] * acc
    o_ref[...] = acc.astype(o_ref.dtype)


def matmul(x, w, *, bias=None, in_act=None, out_act=None, res=None, gate=None, rows_per_gate=None,
           out_dtype=F32, tm=1024, tn=512, layer=None, x2=None, w2=None, name="matmul"):
    m, k = x.shape[-2:]
    k2, n = w.shape[-2:]
    assert k == k2
    has_res = res is not None
    tm = _tile(rows_per_gate if has_res else m, tm, SUBLANES)
    tn = _tile(n, tn, LANES)
    assert m % tm == 0
    x_spec = (pl.BlockSpec((tm, k), lambda j, i: (i, 0)) if x.ndim == 2
              else pl.BlockSpec((None, tm, k), lambda j, i: (layer, i, 0)))
    w_spec = (pl.BlockSpec((k, tn), lambda j, i: (0, j)) if w.ndim == 2
              else pl.BlockSpec((None, k, tn), lambda j, i: (layer, 0, j)))
    in_specs = [x_spec, w_spec]
    args = [x, w]
    if x2 is not None:
        in_specs += [pl.BlockSpec((tm, x2.shape[1]), lambda j, i: (i, 0)),
                     pl.BlockSpec((x2.shape[1], tn), lambda j, i: (0, j))]
        args += [x2, w2]
    if bias is not None:
        in_specs.append(pl.BlockSpec((1, tn), lambda j, i: (0, j)))
        args.append(bias)
    if has_res:
        rpg = rows_per_gate // tm
        in_specs.append(pl.BlockSpec((tm, tn), lambda j, i: (i, j)))
        in_specs.append(pl.BlockSpec((None, 1, tn), lambda j, i: (i // rpg, 0, j)))
        args += [res, gate]
    return pl.pallas_call(
        functools.partial(_mm_kernel, in_act=in_act, out_act=out_act, has_bias=bias is not None, has_res=has_res,
                          has_x2=x2 is not None),
        out_shape=jax.ShapeDtypeStruct((m, n), out_dtype),
        grid=(n // tn, m // tm),
        in_specs=in_specs,
        out_specs=pl.BlockSpec((tm, tn), lambda j, i: (i, j)),
        compiler_params=_cparams("parallel", "parallel"),
        name=name,
    )(*args)


def _norm_kernel(*refs, has_res, has_mod, emit_x, transposed):
    pos = 0
    x = refs[pos][...]
    pos += 1
    if has_res:
        x = x + refs[pos + 1][...] * refs[pos][...]
        pos += 2
    g = refs[pos][...]
    pos += 1
    y = x * lax.rsqrt(jnp.mean(x * x, axis=-1, keepdims=True) + NORM_EPS) * g
    if has_mod:
        y = y * (1.0 + refs[pos + 1][...]) + refs[pos][...]
        pos += 2
    if emit_x:
        refs[pos][...] = x
        pos += 1
    if transposed:
        y = y.T
    refs[pos][...] = y.astype(refs[pos].dtype)


def norm_mod(x, g, *, shift=None, scale=None, res=None, gate=None, emit_x=False, out_dtype=BF16, tl=256,
             transposed=False):
    b, l, d = x.shape
    tl = _tile(l, tl, LANES if transposed else SUBLANES)
    nl = l // tl
    row = pl.BlockSpec((None, tl, d), lambda bi, i: (bi, i, 0))
    vec = pl.BlockSpec((None, 1, d), lambda bi, i: (bi, 0, 0))
    in_specs, args = [row], [x]
    if res is not None:
        in_specs += [row, vec]
        args += [res, gate]
    in_specs.append(pl.BlockSpec((1, d), lambda bi, i: (0, 0)))
    args.append(g)
    if shift is not None:
        in_specs += [vec, vec]
        args += [shift, scale]
    if transposed:
        out_shape = [jax.ShapeDtypeStruct((d, b * l), out_dtype)]
        out_specs = [pl.BlockSpec((d, tl), lambda bi, i: (0, bi * nl + i))]
    else:
        out_shape = [jax.ShapeDtypeStruct((b, l, d), out_dtype)]
        out_specs = [row]
    if emit_x:
        out_shape.insert(0, jax.ShapeDtypeStruct((b, l, d), F32))
        out_specs.insert(0, row)
    outs = pl.pallas_call(
        functools.partial(_norm_kernel, has_res=res is not None, has_mod=shift is not None, emit_x=emit_x,
                          transposed=transposed),
        out_shape=out_shape,
        grid=(b, nl),
        in_specs=in_specs,
        out_specs=out_specs,
        compiler_params=_cparams("parallel", "parallel"),
        name="norm_mod",
    )(*args)
    return outs if emit_x else outs[0]


def _rope(x, cos2, sin2):
    return x * cos2 + pltpu.roll(x, ATT_HEAD_DIM // 2, axis=1) * sin2


def _attn_kernel(sink_ref, q_ref, kp_ref, kc_ref, kn_ref, vp_ref, vc_ref, vn_ref,
                 cq_ref, sq_ref, cp_ref, sp_ref, cn_ref, sn_ref, o_ref, *, seq_len):
    n = pl.program_id(1)
    h = pl.program_id(2)
    blk = WINDOW
    k3 = jnp.concatenate([_rope(kp_ref[...], cp_ref[...], sp_ref[...]),
                          _rope(kc_ref[...], cq_ref[...], sq_ref[...]),
                          _rope(kn_ref[...], cn_ref[...], sn_ref[...])], axis=0).astype(BF16)
    v3 = jnp.concatenate([vp_ref[...], vc_ref[...], vn_ref[...]], axis=0).astype(BF16)
    qpos = n * blk + lax.broadcasted_iota(jnp.int32, (blk, 3 * blk), 0)
    kpos = (n - 1) * blk + lax.broadcasted_iota(jnp.int32, (blk, 3 * blk), 1)
    valid = (jnp.abs(qpos - kpos) <= WINDOW) & (kpos >= 0) & (kpos < seq_len)
    heads = range(ATT_GROUP)
    qs = [_rope(q_ref[:, g * ATT_HEAD_DIM:(g + 1) * ATT_HEAD_DIM], cq_ref[...], sq_ref[...]) for g in heads]
    ss = [jnp.where(valid, _bdot_nt(q, k3) * (ATT_HEAD_DIM ** -0.5), -jnp.inf) for q in qs]
    ps = []
    for g in heads:
        sink = sink_ref[h * ATT_GROUP + g]
        m = jnp.maximum(jnp.max(ss[g], axis=-1, keepdims=True), sink)
        p = jnp.exp(ss[g] - m)
        ps.append(p / (jnp.sum(p, axis=-1, keepdims=True) + jnp.exp(sink - m)))
    for g in heads:
        o_ref[:, g * ATT_HEAD_DIM:(g + 1) * ATT_HEAD_DIM] = _bdot(ps[g], v3).astype(o_ref.dtype)


def banded_attention(z, sinks, q_w, kv_w):
    b, l, _ = z.shape
    blk = WINDOW
    nb = l // blk
    hd = ATT_HEAD_DIM
    kvh = kv_w // hd
    gw = ATT_GROUP * hd
    half = hd // 2
    inv = ROPE_THETA ** (-jnp.arange(half, dtype=F32) * 2.0 / hd)
    ang = jnp.arange(l, dtype=F32)[:, None] * inv[None, :]
    cos2 = jnp.concatenate([jnp.cos(ang), jnp.cos(ang)], axis=-1)
    sin2 = jnp.concatenate([-jnp.sin(ang), jnp.sin(ang)], axis=-1)
    kcol, vcol = q_w // hd, (q_w + kv_w) // hd

    def prev(i):
        return jnp.maximum(i - 1, 0)

    def nxt(i):
        return jnp.minimum(i + 1, nb - 1)

    def kv_spec(col0, which):
        return pl.BlockSpec((None, blk, hd), lambda bi, i, h: (bi, which(i), col0 + h))

    def tab_spec(which):
        return pl.BlockSpec((blk, hd), lambda bi, i, h: (which(i), 0))

    same = lambda i: i
    return pl.pallas_call(
        functools.partial(_attn_kernel, seq_len=l),
        out_shape=jax.ShapeDtypeStruct((b, l, q_w), BF16),
        grid=(b, nb, kvh),
        in_specs=[pl.BlockSpec(memory_space=pltpu.SMEM),
                  pl.BlockSpec((None, blk, gw), lambda bi, i, h: (bi, i, h)),
                  kv_spec(kcol, prev), kv_spec(kcol, same), kv_spec(kcol, nxt),
                  kv_spec(vcol, prev), kv_spec(vcol, same), kv_spec(vcol, nxt),
                  tab_spec(same), tab_spec(same), tab_spec(prev), tab_spec(prev), tab_spec(nxt), tab_spec(nxt)],
        out_specs=pl.BlockSpec((None, blk, gw), lambda bi, i, h: (bi, i, h)),
        compiler_params=_cparams("parallel", "parallel", "parallel"),
        name="banded_attention",
    )(sinks, z, z, z, z, z, z, z, cos2, sin2, cos2, sin2, cos2, sin2)


def _shift_rows(x, prev_row, next_row):
    tl = x.shape[0]
    row = lax.broadcasted_iota(jnp.int32, x.shape, 0)
    up = jnp.where(row == 0, prev_row, pltpu.roll(x, 1, axis=0))
    dn = jnp.where(row == tl - 1, next_row, pltpu.roll(x, tl - 1, axis=0))
    return up, dn


def _halo_rows(prev_ref, next_ref):
    i = pl.program_id(1)
    last = pl.num_programs(1) - 1
    prev_row = jnp.where(i == 0, 0.0, prev_ref[SUBLANES - 1:SUBLANES, :])
    next_row = jnp.where(i == last, 0.0, next_ref[0:1, :])
    return prev_row, next_row


def _hy_pro_kernel(*refs):
    groups = [refs[3 * g:3 * g + 3] for g in range(3)]
    cw_refs = refs[9:12]
    cb_refs = refs[12:15]
    bias_ref = refs[15]
    w_ref, x0_ref, t2_ref = refs[16:19]
    u = []
    for (c_ref, p_ref, n_ref), cw_ref, cb_ref in zip(groups, cw_refs, cb_refs):
        x = c_ref[...]
        prev_row, next_row = _halo_rows(p_ref, n_ref)
        up, dn = _shift_rows(x, prev_row, next_row)
        u.append(up * cw_ref[0:1, :] + x * cw_ref[1:2, :] + dn * cw_ref[2:3, :] + cb_ref[...])
    x0, x1, hv = u
    w = hv * x1
    w_ref[...] = w.astype(w_ref.dtype)
    x0_ref[...] = x0.astype(x0_ref.dtype)
    t2_ref[...] = (x0 * (w * bias_ref[...])).astype(t2_ref.dtype)


def hyena_prologue(z, conv_w, conv_b, hy_bias, col0, c):
    b, l, _ = z.shape
    tl = _tile(l, 256, SUBLANES)
    tc = _tile(c, 512, LANES)
    nh = l // SUBLANES
    tps = tl // SUBLANES
    in_specs, args = [], []
    for g in range(3):
        cb0 = (col0 + g * c) // tc
        in_specs += [
            pl.BlockSpec((None, tl, tc), lambda bi, i, j, cb0=cb0: (bi, i, cb0 + j)),
            pl.BlockSpec((None, SUBLANES, tc), lambda bi, i, j, cb0=cb0: (bi, jnp.maximum(i * tps - 1, 0), cb0 + j)),
            pl.BlockSpec((None, SUBLANES, tc), lambda bi, i, j, cb0=cb0: (bi, jnp.minimum((i + 1) * tps, nh - 1), cb0 + j)),
        ]
        args += [z, z, z]
    for g in range(3):
        in_specs.append(pl.BlockSpec((3, tc), lambda bi, i, j, g=g: (0, g * (c // tc) + j)))
        args.append(conv_w)
    for g in range(3):
        in_specs.append(pl.BlockSpec((1, tc), lambda bi, i, j, g=g: (0, g * (c // tc) + j)))
        args.append(conv_b)
    in_specs.append(pl.BlockSpec((1, tc), lambda bi, i, j: (0, j)))
    args.append(hy_bias)
    out_spec = pl.BlockSpec((None, tl, tc), lambda bi, i, j: (bi, i, j))
    return pl.pallas_call(
        _hy_pro_kernel,
        out_shape=[jax.ShapeDtypeStruct((b, l, c), F32)] * 3,
        grid=(b, l // tl, c // tc),
        in_specs=in_specs,
        out_specs=[out_spec] * 3,
        compiler_params=_cparams("parallel", "parallel", "parallel"),
        name="hyena_prologue",
    )(*args)


def _hy_filter_kernel(z_ref, w1_ref, b1_ref, f1_ref, w2_ref, b2_ref, f2_ref, w3_ref, dl_ref,
                      h_ref, asum_ref, row0_ref, *, seq_len):
    i = pl.program_id(0)
    tl = z_ref.shape[0]
    c = dl_ref.shape[1]
    h1 = jnp.sin(f1_ref[...] * (_hdot(z_ref[...], w1_ref[...]) + b1_ref[...]))
    h2 = jnp.sin(f2_ref[...] * (_hdot(h1, w2_ref[...]) + b2_ref[...]))
    h3 = _hdot(h2, w3_ref[...])
    row = i * tl + lax.broadcasted_iota(jnp.int32, (tl, 1), 0)
    t = row.astype(F32) * (1.0 / (seq_len - 1))
    window = jnp.exp(-t * dl_ref[...]) + HY_MOD_SHIFT
    h3 = h3 * jnp.concatenate([window, window], axis=1)
    h_ref[...] = h3

    @pl.when(i == 0)
    def _():
        asum_ref[...] = jnp.zeros_like(asum_ref)
        row0_ref[...] = h3[0:1, :]

    asum_ref[...] += jnp.sum(jnp.where(row == 0, 0.0, jnp.abs(h3)), axis=0, keepdims=True)


def hyena_filter(l, w1, b1, f1, w2, b2, f2, w3, c):
    t = jnp.linspace(0.0, 1.0, l, dtype=F32)[:, None]
    w = 2.0 * math.pi * jnp.arange(l, dtype=F32) / l
    bands = jnp.linspace(1e-4, HY_BANDS - 1, HY_BANDS, dtype=F32)
    ang = w[:, None] * bands[None, :]
    z = jnp.concatenate([t, jnp.cos(ang), -jnp.sin(ang)], axis=-1)
    emb = z.shape[1]
    emb_pad = -(-emb // SUBLANES) * SUBLANES
    z = jnp.pad(z, ((0, 0), (0, emb_pad - emb)))
    w1 = jnp.pad(w1, ((0, emb_pad - emb), (0, 0)))
    max_decay = math.log(HY_DECAY_TARGET) / HY_FAST_DECAY
    min_decay = math.log(HY_DECAY_TARGET) / HY_SLOW_DECAY
    deltas = jnp.abs(jnp.linspace(min_decay, max_decay, c, dtype=F32))[None, :]
    ffn = w2.shape[0]
    tl = _tile(l, 512, SUBLANES)
    full = lambda shape: pl.BlockSpec(shape, lambda i: (0, 0))
    h, asum, row0 = pl.pallas_call(
        functools.partial(_hy_filter_kernel, seq_len=l),
        out_shape=[jax.ShapeDtypeStruct((l, 2 * c), F32), jax.ShapeDtypeStruct((1, 2 * c), F32),
                   jax.ShapeDtypeStruct((1, 2 * c), F32)],
        grid=(l // tl,),
        in_specs=[pl.BlockSpec((tl, emb_pad), lambda i: (i, 0)), full((emb_pad, ffn)), full((1, ffn)), full((1, ffn)),
                  full((ffn, ffn)), full((1, ffn)), full((1, ffn)), full((ffn, 2 * c)), full((1, c))],
        out_specs=[pl.BlockSpec((tl, 2 * c), lambda i: (i, 0)), full((1, 2 * c)), full((1, 2 * c))],
        compiler_params=_cparams("arbitrary"),
        name="hyena_filter",
    )(z, w1, b1, f1, w2, b2, f2, w3, deltas)
    norm = asum[:, :c] + asum[:, c:] + jnp.abs(row0[:, :c] + row0[:, c:])
    return h, norm


def _dft_tables(n1, n2):
    n = n1 * n2
    n1h = n1 // 2
    th1 = 2.0 * np.pi * np.outer(np.arange(n1), np.arange(n1h)) / n1
    f1 = np.concatenate([np.cos(th1), -np.sin(th1)], axis=0)
    th1i = 2.0 * np.pi * np.outer(np.arange(n1h), np.arange(n1)) / n1
    f1inv_re, f1inv_im = np.cos(th1i), -np.sin(th1i)
    k1 = np.arange(n1)[:, None, None]
    k2 = np.arange(n2)[None, :, None]
    m2 = np.arange(n2)[None, None, :]
    th = 2.0 * np.pi * (m2 * k2 / n2 + m2 * k1 / n)
    gr, gi = np.cos(th), -np.sin(th)
    g = np.concatenate([np.concatenate([gr, -gi], axis=2), np.concatenate([gi, gr], axis=2)], axis=1)
    grt, git = np.swapaxes(gr, 1, 2), np.swapaxes(gi, 1, 2)
    ginv = np.concatenate([np.concatenate([grt, git], axis=2), np.concatenate([-git, grt], axis=2)], axis=1)
    to = lambda a: jnp.asarray(a.astype(np.float32)).astype(BF16)
    return to(f1), (to(f1inv_re), to(f1inv_im)), to(g), to(ginv)


def _dft_a_kernel(w_ref, x_ref, o_ref):
    xt = pltpu.einshape("msc->smc", x_ref[...])
    out = [_bdot(w_ref[...], xt[s]) for s in range(x_ref.shape[1])]
    o_ref[...] = pltpu.einshape("smc->msc", jnp.stack(out, axis=0))


def dft_stage_a(w, x):
    b, k, n2, c = x.shape
    m = w.shape[0]
    tc = _tile(c, max(LANES, DFT_BLOCK_BYTES // (m * SUBLANES * 4)), LANES)
    return pl.pallas_call(
        _dft_a_kernel,
        out_shape=jax.ShapeDtypeStruct((b, m, n2, c), F32),
        grid=(b, n2 // SUBLANES, c // tc),
        in_specs=[pl.BlockSpec((m, k), lambda bi, i, j: (0, 0)),
                  pl.BlockSpec((None, k, SUBLANES, tc), lambda bi, i, j: (bi, 0, i, j))],
        out_specs=pl.BlockSpec((None, m, SUBLANES, tc), lambda bi, i, j: (bi, 0, i, j)),
        compiler_params=_cparams("parallel", "parallel", "parallel"),
        name="dft_stage_a",
    )(w, x)


def _dft_mid_fwd_kernel(g_ref, fr_ref, fi_ref, br_ref, bi_ref, hr_ref, hi_ref):
    n2 = fr_ref.shape[0]
    xf = _bdot(g_ref[...], jnp.concatenate([fr_ref[...], fi_ref[...]], axis=0))
    xb = _bdot(g_ref[...], jnp.concatenate([br_ref[...], bi_ref[...]], axis=0))
    hr_ref[...] = xf[:n2] + xb[:n2]
    hi_ref[...] = xf[n2:] - xb[n2:]


def dft_mid_forward(g, a):
    _, _, n1, n2, c2 = a.shape
    c = c2 // 2
    tc = _tile(c, 2048, LANES)
    nct = c // tc
    a_spec = lambda ri, off: pl.BlockSpec((None, None, None, n2, tc), lambda k, j: (0, ri, k, 0, off + j))
    o_spec = pl.BlockSpec((None, n2, tc), lambda k, j: (k, 0, j))
    spec = jax.ShapeDtypeStruct((n1, n2, c), F32)
    return pl.pallas_call(
        _dft_mid_fwd_kernel,
        out_shape=[spec, spec],
        grid=(n1, nct),
        in_specs=[pl.BlockSpec((None, 2 * n2, 2 * n2), lambda k, j: (k, 0, 0)),
                  a_spec(0, 0), a_spec(1, 0), a_spec(0, nct), a_spec(1, nct)],
        out_specs=[o_spec, o_spec],
        compiler_params=_cparams("parallel", "parallel"),
        name="dft_mid_forward",
    )(g, a, a, a, a)


def _dft_mid_kernel(g_ref, gi_ref, ar_ref, ai_ref, hr_ref, hi_ref, br_ref, bi_ref):
    n2 = ar_ref.shape[0]
    x = _bdot(g_ref[...], jnp.concatenate([ar_ref[...], ai_ref[...]], axis=0))
    xr, xi = x[:n2], x[n2:]
    hr, hi = hr_ref[...], hi_ref[...]
    y = jnp.concatenate([xr * hr - xi * hi, xr * hi + xi * hr], axis=0)
    bm = _bdot(gi_ref[...], y)
    br_ref[...] = bm[:n2].astype(br_ref.dtype)
    bi_ref[...] = bm[n2:].astype(bi_ref.dtype)


def dft_mid(g, ginv, a, hr, hi):
    b, _, n1, n2, c = a.shape
    tc = _tile(c, 2048, LANES)
    a_spec = lambda ri: pl.BlockSpec((None, None, None, n2, tc), lambda k, bi, j: (bi, ri, k, 0, j))
    h_spec = pl.BlockSpec((None, n2, tc), lambda k, bi, j: (k, 0, j))
    g_spec = pl.BlockSpec((None, 2 * n2, 2 * n2), lambda k, bi, j: (k, 0, 0))
    o_spec = pl.BlockSpec((None, None, n2, tc), lambda k, bi, j: (bi, k, 0, j))
    out = jax.ShapeDtypeStruct((b, n1, n2, c), F32)
    return pl.pallas_call(
        _dft_mid_kernel,
        out_shape=[out, out],
        grid=(n1, b, c // tc),
        in_specs=[g_spec, g_spec, a_spec(0), a_spec(1), h_spec, h_spec],
        out_specs=[o_spec, o_spec],
        compiler_params=_cparams("parallel", "parallel", "parallel"),
        name="dft_mid",
    )(g, ginv, a, a, hr, hi)


def _dft_out_kernel(fr_ref, fi_ref, br_ref, bi_ref, x0_ref, t2_ref, sc_ref, o_ref):
    brt = pltpu.einshape("msc->smc", br_ref[...])
    bit = pltpu.einshape("msc->smc", bi_ref[...])
    y = jnp.stack([_bdot(fr_ref[...], brt[s]) + _bdot(fi_ref[...], bit[s]) for s in range(o_ref.shape[1])], axis=0)
    o_ref[...] = x0_ref[...] * (pltpu.einshape("smc->msc", y) * sc_ref[...]) + t2_ref[...]


def dft_out(f1inv, br, bi, x0, t2, scale):
    b, n1, n2, c = br.shape
    n1h = f1inv[0].shape[0]
    tc = _tile(c, max(LANES, DFT_BLOCK_BYTES // (2 * n1 * SUBLANES * 4)), LANES)
    io = pl.BlockSpec((None, n1h, SUBLANES, tc), lambda bi_, i, j: (bi_, 0, i, j))
    bspec = pl.BlockSpec((None, n1, SUBLANES, tc), lambda bi_, i, j: (bi_, 0, i, j))
    fspec = pl.BlockSpec((n1h, n1), lambda bi_, i, j: (0, 0))
    return pl.pallas_call(
        _dft_out_kernel,
        out_shape=jax.ShapeDtypeStruct((b, n1h, n2, c), F32),
        grid=(b, n2 // SUBLANES, c // tc),
        in_specs=[fspec, fspec, bspec, bspec, io, io, pl.BlockSpec((1, tc), lambda bi_, i, j: (0, j))],
        out_specs=io,
        compiler_params=_cparams("parallel", "parallel", "parallel"),
        name="dft_stage_a_inverse",
    )(f1inv[0], f1inv[1], br, bi, x0, t2, scale)


def hyena_long_conv(w, x0, t2, hfilt, norm):
    b, l, c = w.shape
    n2 = DFT_N2
    n1 = 2 * l // n2
    n1h = n1 // 2
    f1, f1inv, g, ginv = _dft_tables(n1, n2)
    ha = dft_stage_a(f1, hfilt.reshape(1, n1h, n2, 2 * c))
    hr, hi = dft_mid_forward(g, ha.reshape(1, 2, n1, n2, 2 * c))
    a = dft_stage_a(f1, w.reshape(b, n1h, n2, c))
    br, bi = dft_mid(g, ginv, a.reshape(b, 2, n1, n2, c), hr, hi)
    scale = 1.0 / (norm * (2.0 * l))
    y = dft_out(f1inv, br, bi, x0.reshape(b, n1h, n2, c), t2.reshape(b, n1h, n2, c), scale)
    return y.reshape(b, l, c)


def even_mixer(h, z_w, p, i, q_w):
    b, l, d = h.shape
    c = d - q_w
    in_w = z_w.shape[1]
    kv_w = (in_w - q_w - 3 * c) // 2
    z = matmul(h.reshape(b * l, d), z_w, name="even_in_proj").reshape(b, l, in_w)
    att = banded_attention(z, p["ev_sinks"][i], q_w, kv_w)
    w16, x0, t2 = hyena_prologue(z, p["ev_conv_w"][i], p["ev_conv_b"][i][None], p["ev_hy_bias"][i][None],
                                 q_w + 2 * kv_w, c)
    hfilt, norm = hyena_filter(l, p["ev_filt_w1"][i], p["ev_filt_b1"][i][None], p["ev_filt_f1"][i][None],
                               p["ev_filt_w2"][i], p["ev_filt_b2"][i][None], p["ev_filt_f2"][i][None],
                               p["ev_filt_w3"][i], c)
    y_hy = hyena_long_conv(w16, x0, t2, hfilt, norm)
    return att.reshape(b * l, q_w), y_hy.reshape(b * l, c)


def _rw_mix_kernel(h_ref, p_ref, n_ref, lerp_ref, *o_refs):
    h = h_ref[...]
    prev_row, next_row = _halo_rows(p_ref, n_ref)
    up, dn = _shift_rows(h, prev_row, next_row)
    xx = 0.5 * (up + dn) - h
    for n, o_ref in enumerate(o_refs):
        o_ref[...] = (h + xx * lerp_ref[n:n + 1, :]).astype(o_ref.dtype)


def rwkv_mix(h, lerp):
    b, l, d = h.shape
    n = lerp.shape[0]
    tl = _tile(l, 256, SUBLANES)
    tc = _tile(d, 512, LANES)
    nh = l // SUBLANES
    tps = tl // SUBLANES
    blk = pl.BlockSpec((None, tl, tc), lambda bi, i, j: (bi, i, j))
    return pl.pallas_call(
        _rw_mix_kernel,
        out_shape=[jax.ShapeDtypeStruct((b, l, d), BF16)] * n,
        grid=(b, l // tl, d // tc),
        in_specs=[blk,
                  pl.BlockSpec((None, SUBLANES, tc), lambda bi, i, j: (bi, jnp.maximum(i * tps - 1, 0), j)),
                  pl.BlockSpec((None, SUBLANES, tc), lambda bi, i, j: (bi, jnp.minimum((i + 1) * tps, nh - 1), j)),
                  pl.BlockSpec((n, tc), lambda bi, i, j: (0, j))],
        out_specs=[blk] * n,
        compiler_params=_cparams("parallel", "parallel", "parallel"),
        name="rwkv_mix",
    )(h, h, h, lerp)


def _head_sum(x, m0):
    s0 = jnp.sum(jnp.where(m0, x, 0.0), axis=-1, keepdims=True)
    s1 = jnp.sum(jnp.where(m0, 0.0, x), axis=-1, keepdims=True)
    return jnp.where(m0, s0, s1)


def _wkv_kernel(r_ref, k_ref, v_ref, lw_ref, a_ref, kk_ref, ka_ref, rk_ref, y_ref, bon_ref, s_ref, *, rev):
    @pl.when(pl.program_id(2) == 0)
    def _():
        s_ref[...] = jnp.zeros_like(s_ref)

    pair = 2 * RW_HEAD
    pairs = range(s_ref.shape[0])
    c = r_ref.shape[0]
    c2 = 2 * c
    prep = [_wkv_prep(*(ref[:, g * pair:(g + 1) * pair] for ref in
                        (r_ref, k_ref, v_ref, lw_ref, a_ref, kk_ref, ka_ref, rk_ref)), rev) for g in pairs]
    for g in pairs:
        bon_ref[:, g * pair:(g + 1) * pair] = prep[g]["bonus"]
    state = [s_ref[g] for g in pairs]

    row2 = lax.broadcasted_iota(jnp.int32, (c2, c2), 0)
    col2 = lax.broadcasted_iota(jnp.int32, (c2, c2), 1)
    strict = (col2 > row2) if rev else (col2 < row2)
    incl = (col2 >= row2) if rev else (col2 <= row2)
    m_all = [_bdot_nt(jnp.concatenate([p["aq"], p["rq"]], axis=0), jnp.concatenate([p["bd"], p["kd"]], axis=0))
             for p in prep]
    m_ab = [jnp.where(strict, m[:c2, :c2], 0.0) for m in m_all]
    m_ak = [jnp.where(strict, m[:c2, c2:], 0.0) for m in m_all]
    m_rb = [jnp.where(incl, m[c2:, :c2], 0.0) for m in m_all]
    m_rk = [jnp.where(incl, m[c2:, c2:], 0.0) for m in m_all]

    eye = jnp.where(row2 == col2, 1.0, 0.0)
    x = [eye + m for m in m_ab]
    pw = m_ab
    for _ in range(int(math.log2(c)) - 1):
        pw = [_bdot(p, p) for p in pw]
        x = [xi + _bdot(xi, p) for xi, p in zip(x, pw)]

    ar = [_bdot_nt(jnp.concatenate([p["aq"], p["rq"]], axis=0), s) for p, s in zip(prep, state)]
    mv = [_bdot(jnp.concatenate([mk, mr], axis=0), p["v"]) for p, mk, mr in zip(prep, m_ak, m_rk)]
    part = [a + b for a, b in zip(ar, mv)]
    us = [_bdot(xi, t[:c2]) for xi, t in zip(x, part)]
    ys = [t[c2:] + _bdot(m, u) for t, m, u in zip(part, m_rb, us)]
    upd = [lax.dot_general(jnp.concatenate([p["v"], u], axis=0).astype(BF16),
                           jnp.concatenate([p["kc"], p["bc"]], axis=0).astype(BF16),
                           (((0,), (0,)), ((), ())), preferred_element_type=F32) for p, u in zip(prep, us)]
    for g in pairs:
        y_ref[:, g * pair:(g + 1) * pair] = ys[g][:c] + ys[g][c:]
        s_ref[g] = state[g] * prep[g]["decay"] + upd[g]


def _wkv_prep(r, k, v, lw, a, k_k, k_a, r_k, rev):
    c = r.shape[0]
    m0 = lax.broadcasted_iota(jnp.int32, r.shape, 1) < RW_HEAD
    kk = k * k_k
    kk = kk / jnp.maximum(jnp.sqrt(_head_sum(kk * kk, m0)), 1e-12)
    kd = k * (1.0 + (a - 1.0) * k_a)
    bvec = kk * a

    row = lax.broadcasted_iota(jnp.int32, (c, c), 0)
    col = lax.broadcasted_iota(jnp.int32, (c, c), 1)
    tri = jnp.where((col >= row) if rev else (col <= row), 1.0, 0.0)
    hi = lw.astype(BF16)
    rest = lw - hi.astype(F32)
    mid = rest.astype(BF16)
    ci = _bdot(tri, hi) + _bdot(tri, mid) + _bdot(tri, rest - mid.astype(F32))
    ctot = jnp.sum(lw, axis=0, keepdims=True)
    inv = jnp.exp(-ci)
    tail = jnp.exp(ctot - ci)

    def stack(x):
        return jnp.concatenate([jnp.where(m0, x, 0.0), jnp.where(m0, 0.0, x)], axis=0)

    return dict(aq=stack(-kk * jnp.exp(ci - lw)), rq=stack(r * jnp.exp(ci)), kd=stack(kd * inv), bd=stack(bvec * inv),
                kc=stack(kd * tail), bc=stack(bvec * tail), v=stack(v), decay=jnp.exp(ctot),
                bonus=_head_sum(r * kd * r_k, m0) * v)


def wkv_scan(r, k, v, lw, a, k_k, k_a, r_k, rev):
    b, l, d = r.shape
    c = RW_CHUNK
    nch = l // c
    pair = 2 * RW_HEAD
    npairs = d // pair
    group = next(g for g in (WKV_PAIRS_PER_STEP, 3, 2, 1) if npairs % g == 0)
    wide = group * pair
    cidx = (lambda ci: nch - 1 - ci) if rev else (lambda ci: ci)
    seq = pl.BlockSpec((None, c, wide), lambda bi, hp, ci: (bi, cidx(ci), hp))
    par = pl.BlockSpec((1, wide), lambda bi, hp, ci: (0, hp))
    out = jax.ShapeDtypeStruct((b, l, d), F32)
    return pl.pallas_call(
        functools.partial(_wkv_kernel, rev=rev),
        out_shape=[out, out],
        grid=(b, npairs // group, nch),
        in_specs=[seq] * 5 + [par] * 3,
        out_specs=[seq, seq],
        scratch_shapes=[pltpu.VMEM((group, pair, pair), F32)],
        compiler_params=_cparams("parallel", "parallel", "arbitrary"),
        name="wkv_scan_bwd" if rev else "wkv_scan_fwd",
    )(r, k, v, lw, a, k_k, k_a, r_k)


def _rw_post_kernel(yf_ref, yb_ref, bf_ref, bb_ref, g_ref, gg_ref, gb_ref, o_ref):
    pair = 2 * RW_HEAD
    m0 = lax.broadcasted_iota(jnp.int32, (yf_ref.shape[0], pair), 1) < RW_HEAD
    for c0 in range(0, yf_ref.shape[1], pair):
        sl = slice(c0, c0 + pair)
        y = yf_ref[:, sl] + yb_ref[:, sl]
        mu = _head_sum(y, m0) * (1.0 / RW_HEAD)
        dlt = y - mu
        var = _head_sum(dlt * dlt, m0) * (1.0 / RW_HEAD)
        yn = dlt * lax.rsqrt(var + RW_GN_EPS) * gg_ref[:, sl] + gb_ref[:, sl] + bf_ref[:, sl] + bb_ref[:, sl]
        o_ref[:, sl] = (yn * g_ref[:, sl]).astype(o_ref.dtype)


def rwkv_post(yf, yb, bf, bb, g, gn_g, gn_b):
    b, l, d = yf.shape
    pair = _tile(d, 1024, 2 * RW_HEAD)
    tl = _tile(l, 256, SUBLANES)
    blk = pl.BlockSpec((None, tl, pair), lambda bi, i, j: (bi, i, j))
    par = pl.BlockSpec((1, pair), lambda bi, i, j: (0, j))
    return pl.pallas_call(
        _rw_post_kernel,
        out_shape=jax.ShapeDtypeStruct((b, l, d), BF16),
        grid=(b, l // tl, d // pair),
        in_specs=[blk] * 5 + [par, par],
        out_specs=blk,
        compiler_params=_cparams("parallel", "parallel", "parallel"),
        name="rwkv_post",
    )(yf, yb, bf, bb, g, gn_g, gn_b)


def rwkv_mixer(h, p, j):
    b, l, d = h.shape
    t = b * l
    xr, xw, xk, xv, xa, xg = (x.reshape(t, d) for x in rwkv_mix(h, p["od_lerp"][j]))
    r = matmul(xr, p["od_w_r"][j], name="rwkv_r")
    k = matmul(xk, p["od_w_k"][j], name="rwkv_k")
    v = matmul(xv, p["od_w_v"][j], name="rwkv_v")
    g = matmul(matmul(xg, p["od_g1"][j], name="rwkv_g1"), p["od_g2"][j], in_act="sigmoid", name="rwkv_g2")
    w1 = jnp.concatenate([p["od_w1"][j][0], p["od_w1"][j][1]], axis=1)
    a1 = jnp.concatenate([p["od_a1"][j][0], p["od_a1"][j][1]], axis=1)
    tw = matmul(xw, w1, name="rwkv_w1")
    ta = matmul(xa, a1, name="rwkv_a1")
    nl = tw.shape[1] // 2
    na = ta.shape[1] // 2
    shp = (b, l, d)
    ys, bons = [], []
    for di, rev in ((0, False), (1, True)):
        lw = matmul(tw[:, di * nl:(di + 1) * nl], p["od_w2"][j][di], bias=p["od_w0"][j][di][None], in_act="tanh",
                    out_act="log_decay", name="rwkv_w2")
        a = matmul(ta[:, di * na:(di + 1) * na], p["od_a2"][j][di], bias=p["od_a0"][j][di][None],
                   out_act="sigmoid", name="rwkv_a2")
        y, bon = wkv_scan(r.reshape(shp), k.reshape(shp), v.reshape(shp), lw.reshape(shp), a.reshape(shp),
                          p["od_k_k"][j][None], p["od_k_a"][j][None], p["od_r_k"][j].reshape(1, d), rev)
        ys.append(y)
        bons.append(bon)
    out = rwkv_post(ys[0], ys[1], bons[0], bons[1], g.reshape(shp), p["od_gn_g"][j][None], p["od_gn_b"][j][None])
    return out.reshape(t, d)


def _top_values(s, k):
    rows = lax.broadcasted_iota(jnp.int32, (k, s.shape[1]), 0)

    def body(i, carry):
        s, vals = carry
        m = jnp.max(s, axis=0, keepdims=True)
        return jnp.where(s >= m, -jnp.inf, s), jnp.where(rows == i, m, vals)

    return lax.fori_loop(0, k, body, (s, jnp.full((k, s.shape[1]), -jnp.inf, F32)))[1]


def _pair_candidates(v1, v2):
    k = v1.shape[0]
    assert k == 2 * SUBLANES
    row = lax.broadcasted_iota(jnp.int32, (SUBLANES, v1.shape[1]), 0)
    groups = [v1[0:1, :] + v2, v1[1:2, :] + v2[0:SUBLANES, :]]
    for i in range(2, SUBLANES):
        groups.append(jnp.where(row < k // (i + 1), v1[i:i + 1, :] + v2[0:SUBLANES, :], -jnp.inf))
    groups.append(v1[SUBLANES:k, :] + v2[0:1, :])
    return jnp.concatenate(groups, axis=0)


def _peer_topk_kernel(q_ref, sub_ref, s_ref, e_ref, tau_ref):
    k = PEER_TOPK
    for h in range(tau_ref.shape[0]):
        s1 = _bdot(sub_ref[2 * h], q_ref[2 * h])
        s2 = _bdot(sub_ref[2 * h + 1], q_ref[2 * h + 1])
        v1 = _top_values(s1, k)
        v2 = _top_values(s2, k)
        top = _top_values(_pair_candidates(v1, v2), k)
        z = jnp.sum(jnp.exp(top - top[0:1, :]), axis=0, keepdims=True)
        s_ref[2 * h] = s1
        s_ref[2 * h + 1] = s2
        e_ref[2 * h] = jnp.exp(s1 - v1[0:1, :])
        e_ref[2 * h + 1] = jnp.exp(s2 - v2[0:1, :]) / z
        tau_ref[h:h + 1, :] = top[k - 1:k, :]


def peer_topk(q_t, sub):
    hp, dk, t = q_t.shape
    nk = sub.shape[1]
    tm = _tile(t, 512, LANES)
    blk = pl.BlockSpec((hp, nk, tm), lambda i: (0, 0, i))
    big = jax.ShapeDtypeStruct((hp, nk, t), F32)
    return pl.pallas_call(
        _peer_topk_kernel,
        out_shape=[big, big, jax.ShapeDtypeStruct((hp // 2, t), F32)],
        grid=(t // tm,),
        in_specs=[pl.BlockSpec((hp, dk, tm), lambda i: (0, 0, i)), pl.BlockSpec((hp, nk, dk), lambda i: (0, 0, 0))],
        out_specs=[blk, blk, pl.BlockSpec((hp // 2, tm), lambda i: (0, i))],
        compiler_params=_cparams("parallel"),
        name="peer_topk",
    )(q_t, sub)


def _peer_gate(s_ref, e_ref, tau_ref, a):
    gate = None
    for h in range(tau_ref.shape[0]):
        s1 = s_ref[2 * h, pl.ds(a, 1), :]
        e1 = e_ref[2 * h, pl.ds(a, 1), :]
        hit = (s1 + s_ref[2 * h + 1]) >= tau_ref[h:h + 1, :]
        w = jnp.where(hit, e1 * e_ref[2 * h + 1], 0.0)
        gate = w if gate is None else gate + w
    return gate


def _gelu(x):
    return 0.5 * x * (1.0 + lax.erf(x * (2.0 ** -0.5)))


def _peer_main_kernel(u_ref, x_ref, v_ref, s_ref, e_ref, tau_ref, o_ref, *, na):
    j = pl.program_id(1)
    nk = s_ref.shape[1]
    ts = na * nk

    @pl.when(j == 0)
    def _():
        o_ref[...] = jnp.zeros_like(o_ref)

    acts = [_bdot(u_ref[i * ts:(i + 1) * ts, :], x_ref[...]) for i in range(2)]
    for i in range(2):
        act = _gelu(acts[i])
        pieces = [(_peer_gate(s_ref, e_ref, tau_ref, (2 * j + i) * na + al) * act[al * nk:(al + 1) * nk, :]).astype(BF16)
                  for al in range(na)]
        o_ref[...] += lax.dot_general(jnp.concatenate(pieces, axis=0), v_ref[i * ts:(i + 1) * ts, :],
                                      (((0,), (0,)), ((), ())), preferred_element_type=F32)


def peer_main(u, x_t, v, s, e, tau, layer):
    _, ne, d = u.shape
    t = x_t.shape[1]
    nk = s.shape[1]
    tm = _tile(t, 512, LANES)
    na = PEER_KEYS_PER_SUBTILE
    te = 2 * na * nk
    once = pl.Buffered(1)
    tok = lambda shape: pl.BlockSpec(shape, lambda i, j: (0,) * (len(shape) - 1) + (i,), pipeline_mode=once)
    return pl.pallas_call(
        functools.partial(_peer_main_kernel, na=na),
        out_shape=jax.ShapeDtypeStruct((t, d), F32),
        grid=(t // tm, ne // te),
        in_specs=[pl.BlockSpec((None, te, d), lambda i, j: (layer, j, 0)), tok((d, tm)),
                  pl.BlockSpec((None, te, d), lambda i, j: (layer, j, 0)),
                  tok(s.shape[:2] + (tm,)), tok(e.shape[:2] + (tm,)), tok((tau.shape[0], tm))],
        out_specs=pl.BlockSpec((tm, d), lambda i, j: (i, 0), pipeline_mode=once),
        compiler_params=_cparams("parallel", "arbitrary"),
        name="peer_main",
    )(u, x_t, v, s, e, tau)


def peer(x_t, wq_t, sub, u, v, layer):
    q_t = matmul(wq_t, x_t, layer=layer, name="peer_query")
    hp, nk, dk = sub.shape[1:]
    s, e, tau = peer_topk(q_t.reshape(hp, dk, -1), sub[layer])
    return peer_main(u, x_t, v, s, e, tau, layer)


def _prepare(w):
    p = dict(w)
    for name in ("ev_w_in", "ev_w_out", "od_w_r", "od_w_k", "od_w_v", "od_w_o", "od_w1", "od_w2", "od_a1", "od_a2",
                 "od_g1", "od_g2", "pk_u", "pk_v"):
        p[name] = w[name].astype(BF16)
    p["pk_wq_t"] = jnp.swapaxes(w["pk_w_q"].astype(BF16), 1, 2)
    sk = w["pk_sub_keys"]
    p["pk_sub"] = sk.astype(BF16).reshape(sk.shape[0], sk.shape[1] * sk.shape[2], sk.shape[3], sk.shape[4])
    return p


def _trunk(x, mods, p):
    b, l, d = x.shape
    t = b * l
    depth = p["ada_w"].shape[0]
    q_w = p["ev_sinks"].shape[1] * ATT_HEAD_DIM
    y = gt2 = None
    for layer in range(depth):
        sh1, sc1, gt1, sh2, sc2, gt2_next = mods[layer]
        even = layer % 2 == 0
        g1 = p["norm_g"][layer, 0][None]
        kw = dict(shift=sh1, scale=sc1, out_dtype=BF16 if even else F32)
        if y is None:
            h = norm_mod(x, g1, **kw)
        else:
            x, h = norm_mod(x, g1, res=y, gate=gt2, emit_x=True, **kw)
        kw = dict(res=x.reshape(t, d), gate=gt1, rows_per_gate=l, name="mixer_out_proj")
        if even:
            att, hy = even_mixer(h, p["ev_w_in"][layer // 2], p, layer // 2, q_w)
            w_out = p["ev_w_out"][layer // 2]
            x = matmul(att, w_out[:q_w], x2=hy, w2=w_out[q_w:], **kw)
        else:
            x = matmul(rwkv_mixer(h, p, layer // 2), p["od_w_o"][layer // 2], **kw)
        x = x.reshape(b, l, d)
        h2_t = norm_mod(x, p["norm_g"][layer, 1][None], shift=sh2, scale=sc2, transposed=True)
        y = peer(h2_t, p["pk_wq_t"], p["pk_sub"], p["pk_u"], p["pk_v"], layer)
        y = y.reshape(b, l, d)
        gt2 = gt2_next
    return norm_mod(x, p["final_g"][None], res=y, gate=gt2, out_dtype=F32)


def kernel(x_prompt, x_sample, c_prompt, c_sample, ada_w, ada_b, norm_g, final_g, ev_w_in, ev_sinks, ev_conv_w, ev_conv_b, ev_filt_w1, ev_filt_b1, ev_filt_f1, ev_filt_w2, ev_filt_b2, ev_filt_f2, ev_filt_w3, ev_hy_bias, ev_w_out, od_lerp, od_w_r, od_w_k, od_w_v, od_w_o, od_w0, od_w1, od_w2, od_a0, od_a1, od_a2, od_g1, od_g2, od_k_k, od_k_a, od_r_k, od_gn_g, od_gn_b, pk_w_q, pk_sub_keys, pk_u, pk_v):
    p = _prepare(dict(
        ada_w=ada_w, ada_b=ada_b, norm_g=norm_g, final_g=final_g, ev_w_in=ev_w_in, ev_sinks=ev_sinks,
        ev_conv_w=ev_conv_w, ev_conv_b=ev_conv_b, ev_filt_w1=ev_filt_w1, ev_filt_b1=ev_filt_b1, ev_filt_f1=ev_filt_f1,
        ev_filt_w2=ev_filt_w2, ev_filt_b2=ev_filt_b2, ev_filt_f2=ev_filt_f2, ev_filt_w3=ev_filt_w3,
        ev_hy_bias=ev_hy_bias, ev_w_out=ev_w_out, od_lerp=od_lerp, od_w_r=od_w_r, od_w_k=od_w_k, od_w_v=od_w_v,
        od_w_o=od_w_o, od_w0=od_w0, od_w1=od_w1, od_w2=od_w2, od_a0=od_a0, od_a1=od_a1, od_a2=od_a2, od_g1=od_g1,
        od_g2=od_g2, od_k_k=od_k_k, od_k_a=od_k_a, od_r_k=od_r_k, od_gn_g=od_gn_g, od_gn_b=od_gn_b,
        pk_w_q=pk_w_q, pk_sub_keys=pk_sub_keys, pk_u=pk_u, pk_v=pk_v))
    depth, d = ada_w.shape[0], ada_w.shape[1]
    bp, bs = c_prompt.shape[0], c_sample.shape[0]
    rows = -(-(bp + bs) // SUBLANES) * SUBLANES
    c_all = jnp.pad(jnp.concatenate([c_prompt, c_sample], axis=0), ((0, rows - bp - bs), (0, 0)))
    mods_p, mods_s = [], []
    for layer in range(depth):
        mod = matmul(c_all, ada_w, bias=ada_b[layer][None], in_act="silu", layer=layer, name="adaln")
        parts = jnp.split(mod, 6, axis=-1)
        mods_p.append([m[:bp, None, :] for m in parts])
        mods_s.append([m[bp:bp + bs, None, :] for m in parts])
    return _trunk(x_prompt, mods_p, p), _trunk(x_sample, mods_s, p)
```
